```python
import jax, jax.numpy as jnp
from jax import lax
import numpy as np

D_MODEL = 1024
BATCH = 32
SEQ = 2048
DEPTH = 1

CHUNK = 64
RET_HEADS = 4
RET_DK = 128
RET_DV = 256
RET_QK_WIDTH = RET_HEADS * RET_DK
RET_V_WIDTH = RET_HEADS * RET_DV
LRU_WIDTH = 1024
LRU_BLOCKS = 4
LRU_BLOCK = LRU_WIDTH // LRU_BLOCKS
LRU_CONV = 4
LRU_C = 8.0
LRU_MIN_RAD = 0.9
LRU_MAX_RAD = 0.999
D_FF = 3 * D_MODEL
FFN_CONV = 3
ROPE_BASE = 10000.0
RMS_EPS = 1e-6
GN_EPS = 1e-6

IN_SIZES = (RET_QK_WIDTH, RET_QK_WIDTH, RET_V_WIDTH, RET_V_WIDTH,
            LRU_WIDTH, LRU_WIDTH, D_MODEL, D_MODEL)
D_IN = sum(IN_SIZES)
SPLIT_POINTS = tuple(sum(IN_SIZES[:i + 1]) for i in range(len(IN_SIZES) - 1))

kernel_name = "chunk_causal_retention_rglru_gated_hybrid"


def rms_norm(x, w):
    x32 = x.astype(jnp.float32)
    y = x32 * lax.rsqrt(jnp.mean(x32 * x32, axis=-1, keepdims=True) + RMS_EPS)
    return (y * w.astype(jnp.float32)).astype(x.dtype)


def causal_depthwise_conv(x, w, b):
    k_width, channels = w.shape
    y = lax.conv_general_dilated(
        x, w[:, None, :].astype(x.dtype), window_strides=(1,),
        padding=[(k_width - 1, 0)], dimension_numbers=("NWC", "WIO", "NWC"),
        feature_group_count=channels)
    return y + b.astype(x.dtype)


def rotary(x, positions):
    half = x.shape[-1] // 2
    inv_freq = ROPE_BASE ** (-jnp.arange(half, dtype=jnp.float32) / half)
    ang = positions.astype(jnp.float32)[..., None] * inv_freq
    cos = jnp.cos(ang)[:, :, None, :]
    sin = jnp.sin(ang)[:, :, None, :]
    x32 = x.astype(jnp.float32)
    x1, x2 = x32[..., :half], x32[..., half:]
    return jnp.concatenate([x1 * cos - x2 * sin, x1 * sin + x2 * cos], axis=-1).astype(x.dtype)


def chunkwise_retention(q, k, v):
    bsz, seq, heads, dk = q.shape
    dv = v.shape[-1]
    n_chunks = seq // CHUNK
    log_gamma = jnp.log1p(-jnp.power(2.0, -5.0 - jnp.arange(heads, dtype=jnp.float32)))
    idx = jnp.arange(CHUNK, dtype=jnp.float32)
    dist = jnp.abs(idx[:, None] - idx[None, :])
    intra_decay = jnp.exp(log_gamma[:, None, None] * dist)
    q_decay = jnp.exp(log_gamma[:, None] * (idx + 1.0))
    k_decay = jnp.exp(log_gamma[:, None] * (CHUNK - 1.0 - idx))
    chunk_decay = jnp.exp(log_gamma * CHUNK)

    def to_chunks(t):
        return t.astype(jnp.float32).reshape(bsz, n_chunks, CHUNK, heads, -1).transpose(1, 0, 3, 2, 4)

    qc, kc, vc = to_chunks(q), to_chunks(k), to_chunks(v)

    def step(state, inp):
        qi, ki, vi = inp
        scores = jnp.einsum("bhid,bhjd->bhij", qi, ki) * intra_decay
        out = (jnp.einsum("bhij,bhjv->bhiv", scores, vi)
               + jnp.einsum("bhid,bhdv->bhiv", qi * q_decay[:, :, None], state))
        state = (state * chunk_decay[:, None, None]
                 + jnp.einsum("bhjd,bhjv->bhdv", ki * k_decay[:, :, None], vi))
        return state, out

    state0 = jnp.zeros((bsz, heads, dk, dv), jnp.float32)
    _, out = lax.scan(step, state0, (qc, kc, vc))
    return out.transpose(1, 0, 3, 2, 4).reshape(bsz, seq, heads, dv)


def head_group_norm(o, w):
    bsz, seq, heads, dv = o.shape
    mu = jnp.mean(o, axis=-1, keepdims=True)
    var = jnp.mean(jnp.square(o - mu), axis=-1, keepdims=True)
    y = (o - mu) * lax.rsqrt(var + GN_EPS)
    return y.reshape(bsz, seq, heads * dv) * w.astype(jnp.float32)


def _linear_recurrence_combine(e1, e2):
    a1, b1 = e1
    a2, b2 = e2
    return a1 * a2, a2 * b1 + b2


def rg_lru(x, w_r, b_r, w_i, b_i, lam):
    bsz, seq, width = x.shape
    x32 = x.astype(jnp.float32)
    xb = x32.reshape(bsz, seq, LRU_BLOCKS, LRU_BLOCK)
    r = jax.nn.sigmoid(jnp.einsum("bsni,nij->bsnj", xb, w_r.astype(jnp.float32))
                       + b_r.astype(jnp.float32)).reshape(bsz, seq, width)
    i = jax.nn.sigmoid(jnp.einsum("bsni,nij->bsnj", xb, w_i.astype(jnp.float32))
                       + b_i.astype(jnp.float32)).reshape(bsz, seq, width)
    log_a = -LRU_C * r * jax.nn.softplus(-lam.astype(jnp.float32))
    a = jnp.exp(log_a)
    b = jnp.sqrt(-jnp.expm1(2.0 * log_a)) * (i * x32)
    _, h = lax.associative_scan(_linear_recurrence_combine, (a, b), axis=1)
    return h


def setup_inputs(seed: int = 0) -> dict:
    key = jax.random.key(seed)
    ks = jax.random.split(key, 24)

    def nrm(k, shape, scale):
        return jax.random.normal(k, shape, jnp.float32) * scale

    def gain(k, shape):
        return 1.0 + 0.02 * jax.random.normal(k, shape, jnp.float32)

    x = jax.random.normal(ks[0], (BATCH, SEQ, D_MODEL), jnp.float32)
    start = jax.random.randint(ks[1], (BATCH, 1), 0, 64) * CHUNK
    positions = (start + jnp.arange(SEQ)[None, :]).astype(jnp.int32)

    u = jax.random.uniform(ks[12], (DEPTH, LRU_WIDTH), jnp.float32,
                           LRU_MIN_RAD ** 2, LRU_MAX_RAD ** 2)
    a0 = jnp.sqrt(u)
    lru_lambda = jnp.log(a0) - jnp.log1p(-a0)

    return {
        "x": x,
        "positions": positions,
        "norm1_w": gain(ks[2], (DEPTH, D_MODEL)),
        "w_in": nrm(ks[3], (DEPTH, D_MODEL, D_IN), D_MODEL ** -0.5),
        "merge_gate_b": nrm(ks[4], (DEPTH, 2, D_MODEL), 0.02),
        "ret_gn_w": gain(ks[5], (DEPTH, RET_V_WIDTH)),
        "w_ret_o": nrm(ks[6], (DEPTH, RET_V_WIDTH, D_MODEL), RET_V_WIDTH ** -0.5),
        "lru_conv_w": nrm(ks[7], (DEPTH, LRU_CONV, LRU_WIDTH), LRU_CONV ** -0.5),
        "lru_conv_b": nrm(ks[8], (DEPTH, LRU_WIDTH), 0.02),
        "lru_w_r": nrm(ks[9], (DEPTH, LRU_BLOCKS, LRU_BLOCK, LRU_BLOCK), LRU_BLOCK ** -0.5),
        "lru_b_r": nrm(ks[10], (DEPTH, LRU_BLOCKS, LRU_BLOCK), 0.02),
        "lru_w_i": nrm(ks[11], (DEPTH, LRU_BLOCKS, LRU_BLOCK, LRU_BLOCK), LRU_BLOCK ** -0.5),
        "lru_b_i": nrm(ks[13], (DEPTH, LRU_BLOCKS, LRU_BLOCK), 0.02),
        "lru_lambda": lru_lambda,
        "w_lru_o": nrm(ks[14], (DEPTH, LRU_WIDTH, D_MODEL), LRU_WIDTH ** -0.5),
        "w_out": nrm(ks[15], (DEPTH, D_MODEL, D_MODEL), D_MODEL ** -0.5),
        "norm2_w": gain(ks[16], (DEPTH, D_MODEL)),
        "ffn_w_up": nrm(ks[17], (DEPTH, D_MODEL, 2 * D_FF), D_MODEL ** -0.5),
        "ffn_conv_w": nrm(ks[18], (DEPTH, FFN_CONV, D_FF), FFN_CONV ** -0.5),
        "ffn_conv_b": nrm(ks[19], (DEPTH, D_FF), 0.02),
        "ffn_w_down": nrm(ks[20], (DEPTH, D_FF, D_MODEL), D_FF ** -0.5),
        "norm_f_w": gain(ks[21], (D_MODEL,)),
    }


def reference(x, positions, norm1_w, w_in, merge_gate_b, ret_gn_w, w_ret_o,
              lru_conv_w, lru_conv_b, lru_w_r, lru_b_r, lru_w_i, lru_b_i, lru_lambda,
              w_lru_o, w_out, norm2_w, ffn_w_up, ffn_conv_w, ffn_conv_b, ffn_w_down,
              norm_f_w):
    bsz, seq, _ = x.shape
    for l in range(DEPTH):
        h = rms_norm(x, norm1_w[l])
        proj = h @ w_in[l]
        q, k, v, g_ret, x_lru, y_lru, gate_ret, gate_lru = jnp.split(proj, SPLIT_POINTS, axis=-1)

        q = rotary(q.reshape(bsz, seq, RET_HEADS, RET_DK), positions)
        k = rotary(k.reshape(bsz, seq, RET_HEADS, RET_DK), positions) * (RET_DK ** -0.5)
        o = chunkwise_retention(q, k, v.reshape(bsz, seq, RET_HEADS, RET_DV))
        o = head_group_norm(o, ret_gn_w[l])
        y_a = (o * jax.nn.silu(g_ret.astype(jnp.float32))).astype(x.dtype) @ w_ret_o[l]

        xc = causal_depthwise_conv(x_lru, lru_conv_w[l], lru_conv_b[l])
        hl = rg_lru(xc, lru_w_r[l], lru_b_r[l], lru_w_i[l], lru_b_i[l], lru_lambda[l])
        y_b = (hl * jax.nn.gelu(y_lru.astype(jnp.float32))).astype(x.dtype) @ w_lru_o[l]

        mix = (jax.nn.sigmoid(gate_ret + merge_gate_b[l, 0]) * y_a
               + jax.nn.sigmoid(gate_lru + merge_gate_b[l, 1]) * y_b)
        x = x + mix @ w_out[l]

        h = rms_norm(x, norm2_w[l])
        up = h @ ffn_w_up[l]
        gate, val = jnp.split(up, [D_FF], axis=-1)
        gate = causal_depthwise_conv(gate, ffn_conv_w[l], ffn_conv_b[l])
        x = x + (jax.nn.gelu(gate) * val) @ ffn_w_down[l]
    return rms_norm(x, norm_f_w)
```

```python
import functools
import math

import jax
import jax.numpy as jnp
import numpy as np
from jax import lax
from jax.experimental import pallas as pl
from jax.experimental.pallas import tpu as pltpu

D_MODEL = 1024
CHUNK = 64
RET_HEADS = 4
RET_DK = 128
RET_DV = 256
RET_QK_WIDTH = RET_HEADS * RET_DK
RET_V_WIDTH = RET_HEADS * RET_DV
LRU_WIDTH = 1024
LRU_BLOCKS = 4
LRU_BLOCK = LRU_WIDTH // LRU_BLOCKS
LRU_CONV = 4
LRU_C = 8.0
D_FF = 3 * D_MODEL
FFN_CONV = 3
ROPE_BASE = 10000.0
RMS_EPS = 1e-6
GN_EPS = 1e-6

Q_OFF = 0
K_OFF = Q_OFF + RET_QK_WIDTH
V_OFF = K_OFF + RET_QK_WIDTH
G_OFF = V_OFF + RET_V_WIDTH
XL_OFF = G_OFF + RET_V_WIDTH
YL_OFF = XL_OFF + LRU_WIDTH
GR_OFF = YL_OFF + LRU_WIDTH
GL_OFF = GR_OFF + D_MODEL
D_IN = GL_OFF + D_MODEL

SUBLANES = 8
MIX_TILE = 256
FFN_TILE = 256
FFN_COLS = 1024
VMEM_LIMIT_BYTES = 56 * 1024 * 1024

F32 = jnp.float32
BF16 = jnp.bfloat16


def _dot(a, b):
    return jnp.dot(a, b, preferred_element_type=F32)


def _sigmoid(x):
    return 1.0 / (1.0 + jnp.exp(-x))


def _rms_norm(x, w):
    ms = jnp.mean(x * x, axis=-1, keepdims=True)
    return x * lax.rsqrt(ms + RMS_EPS) * w


def _shift_rows(cur, prev_tail, k):
    rolled = pltpu.roll(cur, k, axis=0)
    row = lax.broadcasted_iota(jnp.int32, (SUBLANES, cur.shape[1]), 0)
    head = jnp.where(row < k, pltpu.roll(prev_tail, k, axis=0), rolled[:SUBLANES])
    return jnp.concatenate([head, rolled[SUBLANES:]], axis=0)


def _causal_conv(cur, prev_tail, w_ref, b_ref):
    k_width = w_ref.shape[0]
    acc = cur * w_ref[k_width - 1:k_width, :] + b_ref[...]
    for k in range(1, k_width):
        acc = acc + _shift_rows(cur, prev_tail, k) * w_ref[k_width - 1 - k:k_width - k, :]
    return acc


def _linear_scan(a, b):
    n_rows = a.shape[0]
    row = lax.broadcasted_iota(jnp.int32, a.shape, 0)
    shift = 1
    while shift < n_rows:
        keep = row >= shift
        a_prev = jnp.where(keep, pltpu.roll(a, shift, axis=0), 1.0)
        b_prev = jnp.where(keep, pltpu.roll(b, shift, axis=0), 0.0)
        b = a * b_prev + b
        a = a * a_prev
        shift *= 2
    return a, b


def _mixer_kernel(x_ref, pos_ref, invf_ref, n1w_ref, win_ref, mgb_ref, gnw_ref, wro_ref,
                  cw_ref, cb_ref, wri_ref, br_ref, bi_ref, lam_ref, wlo_ref, wout_ref,
                  dmask_ref, qdec_ref, kdec_ref, o_ref,
                  state_ref, hcar_ref, xcar_ref, *, state_decay):
    @pl.when(pl.program_id(1) == 0)
    def _():
        state_ref[...] = jnp.zeros_like(state_ref)
        hcar_ref[...] = jnp.zeros_like(hcar_ref)
        xcar_ref[...] = jnp.zeros_like(xcar_ref)

    x = x_ref[...]
    h = _rms_norm(x, n1w_ref[...]).astype(BF16)

    def proj(off, width):
        return _dot(h, win_ref[:, off:off + width])

    ang = pos_ref[...].astype(F32) * invf_ref[...]
    cos = jnp.cos(ang)
    sin = jnp.sin(ang)
    lane = lax.broadcasted_iota(jnp.int32, ang.shape, 1)
    sin = jnp.where(lane < RET_DK // 2, -sin, sin)

    q = proj(Q_OFF, RET_QK_WIDTH)
    k = proj(K_OFF, RET_QK_WIDTH)
    heads = []
    for hd in range(RET_HEADS):
        qh = q[:, hd * RET_DK:(hd + 1) * RET_DK]
        kh = k[:, hd * RET_DK:(hd + 1) * RET_DK]
        qh = qh * cos + pltpu.roll(qh, RET_DK // 2, axis=1) * sin
        kh = (kh * cos + pltpu.roll(kh, RET_DK // 2, axis=1) * sin) * (RET_DK ** -0.5)
        vh = proj(V_OFF + hd * RET_DV, RET_DV).astype(BF16)
        scores = lax.dot_general(qh.astype(BF16), kh.astype(BF16), (((1,), (1,)), ((), ())),
                                 preferred_element_type=F32) * dmask_ref[hd]
        state = state_ref[hd]
        q_dec = (qh * qdec_ref[:, hd * RET_DK:(hd + 1) * RET_DK]).astype(BF16)
        k_dec = (kh * kdec_ref[:, hd * RET_DK:(hd + 1) * RET_DK]).astype(BF16)
        oh = _dot(scores.astype(BF16), vh) + _dot(q_dec, state.astype(BF16))
        state_ref[hd] = state * state_decay[hd] + lax.dot_general(
            k_dec, vh, (((0,), (0,)), ((), ())), preferred_element_type=F32)
        mu = jnp.mean(oh, axis=-1, keepdims=True)
        dev = oh - mu
        var = jnp.mean(dev * dev, axis=-1, keepdims=True)
        heads.append(dev * lax.rsqrt(var + GN_EPS))
    o = jnp.concatenate(heads, axis=1) * gnw_ref[...]
    g = proj(G_OFF, RET_V_WIDTH)
    y_a = _dot((o * (g * _sigmoid(g))).astype(BF16), wro_ref[...])

    xl = proj(XL_OFF, LRU_WIDTH)
    xc = _causal_conv(xl, xcar_ref[...], cw_ref, cb_ref)
    xcar_ref[...] = xl[xl.shape[0] - SUBLANES:, :]
    r_parts, i_parts = [], []
    for blk in range(LRU_BLOCKS):
        ri = _dot(xc[:, blk * LRU_BLOCK:(blk + 1) * LRU_BLOCK].astype(BF16), wri_ref[blk])
        r_parts.append(ri[:, :LRU_BLOCK])
        i_parts.append(ri[:, LRU_BLOCK:])
    r = _sigmoid(jnp.concatenate(r_parts, axis=1) + br_ref[...])
    i = _sigmoid(jnp.concatenate(i_parts, axis=1) + bi_ref[...])
    neg_lam = -lam_ref[...]
    softplus = jnp.maximum(neg_lam, 0.0) + jnp.log1p(jnp.exp(-jnp.abs(neg_lam)))
    log_a = (-LRU_C * r) * softplus
    a = jnp.exp(log_a)
    b = jnp.sqrt(1.0 - a * a) * (i * xc)
    a_cum, b_cum = _linear_scan(a, b)
    hl = b_cum + a_cum * hcar_ref[0:1, :]
    hcar_ref[0:1, :] = hl[hl.shape[0] - 1:, :]
    y_b = _dot((hl * jax.nn.gelu(proj(YL_OFF, LRU_WIDTH))).astype(BF16), wlo_ref[...])

    gate_a = _sigmoid(proj(GR_OFF, D_MODEL) + mgb_ref[0:1, :])
    gate_b = _sigmoid(proj(GL_OFF, D_MODEL) + mgb_ref[1:2, :])
    mix = gate_a * y_a + gate_b * y_b
    o_ref[...] = x + _dot(mix.astype(BF16), wout_ref[...])


def _ffn_kernel(x_ref, n2w_ref, wup_ref, cw_ref, cb_ref, wdn_ref, nfw_ref, o_ref, gcar_ref, *,
                final_norm):
    @pl.when(pl.program_id(1) == 0)
    def _():
        gcar_ref[...] = jnp.zeros_like(gcar_ref)

    x = x_ref[...]
    h = _rms_norm(x, n2w_ref[...]).astype(BF16)
    acc = x
    for c in range(D_FF // FFN_COLS):
        cols = slice(c * FFN_COLS, (c + 1) * FFN_COLS)
        gate = _dot(h, wup_ref[:, cols])
        val = _dot(h, wup_ref[:, D_FF + c * FFN_COLS:D_FF + (c + 1) * FFN_COLS])
        conv = _causal_conv(gate, gcar_ref[:, cols], cw_ref.at[:, cols], cb_ref.at[:, cols])
        gcar_ref[:, cols] = gate[gate.shape[0] - SUBLANES:, :]
        acc = acc + _dot((jax.nn.gelu(conv) * val).astype(BF16), wdn_ref[cols, :])
    o_ref[...] = _rms_norm(acc, nfw_ref[...]) if final_norm else acc


def _const_spec(shape):
    zeros = (0,) * len(shape)
    return pl.BlockSpec(shape, lambda b, s: zeros, pipeline_mode=pl.Buffered(1))


def _retention_tables(tile):
    hd = np.arange(RET_HEADS, dtype=np.float64)
    log_gamma = np.log1p(-np.power(2.0, -5.0 - hd))
    idx = np.arange(tile, dtype=np.float64)
    diff = idx[:, None] - idx[None, :]
    chunk = np.arange(tile) // CHUNK
    same = chunk[:, None] == chunk[None, :]
    earlier = chunk[None, :] < chunk[:, None]
    expo = np.where(same, np.abs(diff), diff)
    dmask = np.where(same | earlier, np.exp(log_gamma[:, None, None] * expo), 0.0)
    q_dec = np.exp(log_gamma[None, :] * (idx[:, None] + 1.0))
    k_dec = np.exp(log_gamma[None, :] * (tile - 1.0 - idx[:, None]))
    state_decay = tuple(float(v) for v in np.exp(log_gamma * tile))
    widen = lambda t: np.repeat(t, RET_DK, axis=1).astype(np.float32)
    return dmask.astype(np.float32), widen(q_dec), widen(k_dec), state_decay


def kernel(x, positions, norm1_w, w_in, merge_gate_b, ret_gn_w, w_ret_o, lru_conv_w, lru_conv_b,
           lru_w_r, lru_b_r, lru_w_i, lru_b_i, lru_lambda, w_lru_o, w_out, norm2_w, ffn_w_up,
           ffn_conv_w, ffn_conv_b, ffn_w_down, norm_f_w):
    bsz, seq, d_model = x.shape
    depth = w_in.shape[0]
    assert d_model == D_MODEL and seq % MIX_TILE == 0 and seq % FFN_TILE == 0
    assert MIX_TILE % CHUNK == 0

    half = RET_DK // 2
    inv_freq = ROPE_BASE ** (-jnp.arange(half, dtype=F32) / half)
    inv_freq2 = jnp.concatenate([inv_freq, inv_freq])[None, :]
    pos3 = positions[:, :, None]
    dmask, q_dec, k_dec, state_decay = _retention_tables(MIX_TILE)
    row = lambda v: v.reshape(1, -1)

    params = pltpu.CompilerParams(dimension_semantics=("arbitrary", "arbitrary"),
                                  vmem_limit_bytes=VMEM_LIMIT_BYTES)
    tile_spec = lambda t: pl.BlockSpec((None, t, D_MODEL), lambda b, s: (b, s, 0))

    for l in range(depth):
        w_ri = jnp.concatenate([lru_w_r[l], lru_w_i[l]], axis=-1).astype(BF16)
        mixer_consts = [
            inv_freq2, row(norm1_w[l]), w_in[l].astype(BF16), merge_gate_b[l], row(ret_gn_w[l]),
            w_ret_o[l].astype(BF16), lru_conv_w[l], row(lru_conv_b[l]), w_ri,
            row(lru_b_r[l]), row(lru_b_i[l]), row(lru_lambda[l]), w_lru_o[l].astype(BF16),
            w_out[l].astype(BF16), jnp.asarray(dmask), jnp.asarray(q_dec), jnp.asarray(k_dec)]
        x = pl.pallas_call(
            functools.partial(_mixer_kernel, state_decay=state_decay),
            grid=(bsz, seq // MIX_TILE),
            in_specs=[tile_spec(MIX_TILE),
                      pl.BlockSpec((None, MIX_TILE, 1), lambda b, s: (b, s, 0))]
                     + [_const_spec(c.shape) for c in mixer_consts],
            out_specs=tile_spec(MIX_TILE),
            out_shape=jax.ShapeDtypeStruct(x.shape, x.dtype),
            scratch_shapes=[pltpu.VMEM((RET_HEADS, RET_DK, RET_DV), F32),
                            pltpu.VMEM((SUBLANES, LRU_WIDTH), F32),
                            pltpu.VMEM((SUBLANES, LRU_WIDTH), F32)],
            compiler_params=params,
            name="token_mixer",
        )(x, pos3, *mixer_consts)

        ffn_consts = [row(norm2_w[l]), ffn_w_up[l].astype(BF16), ffn_conv_w[l], row(ffn_conv_b[l]),
                      ffn_w_down[l].astype(BF16), row(norm_f_w)]
        x = pl.pallas_call(
            functools.partial(_ffn_kernel, final_norm=(l == depth - 1)),
            grid=(bsz, seq // FFN_TILE),
            in_specs=[tile_spec(FFN_TILE)] + [_const_spec(c.shape) for c in ffn_consts],
            out_specs=tile_spec(FFN_TILE),
            out_shape=jax.ShapeDtypeStruct(x.shape, x.dtype),
            scratch_shapes=[pltpu.VMEM((SUBLANES, D_FF), F32)],
            compiler_params=params,
            name="channel_mixer",
        )(x, *ffn_consts)
    return x
```

```python
import functools

import jax
import jax.numpy as jnp
import numpy as np
from jax import lax
from jax.experimental import pallas as pl
from jax.experimental.pallas import tpu as pltpu

D_MODEL = 1024
CHUNK = 64
RET_HEADS = 4
RET_DK = 128
RET_DV = 256
RET_QK_WIDTH = RET_HEADS * RET_DK
RET_V_WIDTH = RET_HEADS * RET_DV
LRU_WIDTH = 1024
LRU_BLOCKS = 4
LRU_BLOCK = LRU_WIDTH // LRU_BLOCKS
LRU_CONV = 4
LRU_C = 8.0
D_FF = 3 * D_MODEL
FFN_CONV = 3
ROPE_BASE = 10000.0
RMS_EPS = 1e-6
GN_EPS = 1e-6

Q_OFF = 0
K_OFF = Q_OFF + RET_QK_WIDTH
V_OFF = K_OFF + RET_QK_WIDTH
G_OFF = V_OFF + RET_V_WIDTH
XL_OFF = G_OFF + RET_V_WIDTH
YL_OFF = XL_OFF + LRU_WIDTH
GR_OFF = YL_OFF + LRU_WIDTH
GL_OFF = GR_OFF + D_MODEL
D_IN = GL_OFF + D_MODEL

SUBLANES = 8
MXU_COLS = 256
MIX_TILE = 256
MERGE_COLS = MXU_COLS
FFN_TILE = 512
FFN_COLS = 1024
VMEM_LIMIT_BYTES = 56 * 1024 * 1024

F32 = jnp.float32
BF16 = jnp.bfloat16


def _dot(a, b):
    return jnp.dot(a, b, preferred_element_type=F32)


def _sigmoid(x):
    return 1.0 / (1.0 + jnp.exp(-x))


def _rms_norm(x, w):
    ms = jnp.mean(x * x, axis=-1, keepdims=True)
    return x * lax.rsqrt(ms + RMS_EPS) * w


def _group(v, j):
    return v[j * SUBLANES:(j + 1) * SUBLANES, :]


def _sublane_index(width):
    return lax.broadcasted_iota(jnp.int32, (SUBLANES, width), 0)


def _shift_rows(cur, prev_tail, k):
    rolled = pltpu.roll(cur, k, axis=0)
    head = jnp.where(_sublane_index(cur.shape[1]) < k, pltpu.roll(prev_tail, k, axis=0),
                     rolled[:SUBLANES])
    return jnp.concatenate([head, rolled[SUBLANES:]], axis=0)


def _shift_segments(cur, prev_tail, k):
    n_rows, width = cur.shape
    n_prev = prev_tail.shape[0]
    first = _sublane_index(width) == 0
    heads = []
    for g in range(k):
        src = cur[n_rows - SUBLANES * (k - g):n_rows - SUBLANES * (k - g - 1), :]
        prev = prev_tail[n_prev - SUBLANES * (k - g):n_prev - SUBLANES * (k - g - 1), :]
        heads.append(jnp.where(first, pltpu.roll(prev, 1, axis=0), pltpu.roll(src, 1, axis=0)))
    return jnp.concatenate(heads + [cur[:n_rows - SUBLANES * k, :]], axis=0)


def _causal_conv(cur, prev_tail, w_ref, b_ref, shift):
    k_width = w_ref.shape[0]
    acc = cur * w_ref[k_width - 1:k_width, :] + b_ref[...]
    for k in range(1, k_width):
        acc = acc + shift(cur, prev_tail, k) * w_ref[k_width - 1 - k:k_width - k, :]
    return acc


def _segment_scan(a, b, h_in):
    n_groups = a.shape[0] // SUBLANES
    width = a.shape[1]
    h = jnp.zeros((SUBLANES, width), F32)
    p = jnp.ones((SUBLANES, width), F32)
    for j in range(n_groups):
        a_j = _group(a, j)
        h = a_j * h + _group(b, j)
        p = p * a_j
    sub = _sublane_index(width)
    shift = 1
    while shift < SUBLANES:
        keep = sub >= shift
        p_prev = jnp.where(keep, pltpu.roll(p, shift, axis=0), 1.0)
        h_prev = jnp.where(keep, pltpu.roll(h, shift, axis=0), 0.0)
        h = p * h_prev + h
        p = p * p_prev
        shift *= 2
    seg_end = h + p * h_in
    h = jnp.where(sub == 0, h_in, pltpu.roll(seg_end, 1, axis=0))
    out = []
    for j in range(n_groups):
        h = _group(a, j) * h + _group(b, j)
        out.append(h)
    return jnp.concatenate(out, axis=0), seg_end[SUBLANES - 1:, :]


def _mixer_kernel(x_ref, pos_ref, perm_ref, unperm_ref, invf_ref, n1w_ref, win_ref, mgb_ref,
                  gnw_ref, wro_ref, cw_ref, cb_ref, wri_ref, br_ref, bi_ref, lam_ref, wlo_ref,
                  wout_ref, dmask_ref, qdec_ref, kdec_ref, o_ref,
                  state_ref, hcar_ref, xcar_ref, *, state_decay):
    @pl.when(pl.program_id(1) == 0)
    def _():
        state_ref[...] = jnp.zeros_like(state_ref)
        hcar_ref[...] = jnp.zeros_like(hcar_ref)
        xcar_ref[...] = jnp.zeros_like(xcar_ref)

    x = x_ref[...]
    h = _rms_norm(x, n1w_ref[...]).astype(BF16)
    h = _dot(perm_ref[...], h).astype(BF16)

    def proj(off, width):
        return _dot(h, win_ref[:, off:off + width])

    ang = pos_ref[...].astype(F32) * invf_ref[...]
    cos = jnp.cos(ang)
    sin = jnp.sin(ang)
    lane = lax.broadcasted_iota(jnp.int32, ang.shape, 1)
    sin = jnp.where(lane < RET_DK // 2, -sin, sin)

    qk = proj(Q_OFF, 2 * RET_QK_WIDTH)

    states = [state_ref[hd] for hd in range(RET_HEADS)]
    h_in = hcar_ref[0:1, :]
    x_tail = xcar_ref[...]
    new_states = [None] * RET_HEADS
    new_h = [None] * LRU_BLOCKS
    new_tail = [None] * LRU_BLOCKS
    a_parts = [None] * RET_HEADS
    b_parts = [None] * LRU_BLOCKS

    def retention_head(hd):
        qh = qk[:, hd * RET_DK:(hd + 1) * RET_DK]
        kh = qk[:, RET_QK_WIDTH + hd * RET_DK:RET_QK_WIDTH + (hd + 1) * RET_DK]
        qh = qh * cos + pltpu.roll(qh, RET_DK // 2, axis=1) * sin
        kh = (kh * cos + pltpu.roll(kh, RET_DK // 2, axis=1) * sin) * (RET_DK ** -0.5)
        vh = proj(V_OFF + hd * RET_DV, RET_DV).astype(BF16)
        yield
        scores = lax.dot_general(qh.astype(BF16), kh.astype(BF16), (((1,), (1,)), ((), ())),
                                 preferred_element_type=F32)
        q_dec = (qh * qdec_ref[:, hd * RET_DK:(hd + 1) * RET_DK]).astype(BF16)
        k_dec = (kh * kdec_ref[:, hd * RET_DK:(hd + 1) * RET_DK]).astype(BF16)
        yield
        scores = (scores * dmask_ref[hd]).astype(BF16)
        g = proj(G_OFF + hd * RET_DV, RET_DV)
        yield
        oh = _dot(scores, vh) + _dot(q_dec, states[hd].astype(BF16))
        new_states[hd] = states[hd] * state_decay[hd] + lax.dot_general(
            k_dec, vh, (((0,), (0,)), ((), ())), preferred_element_type=F32)
        swish = g * _sigmoid(g)
        yield
        mu = jnp.mean(oh, axis=-1, keepdims=True)
        dev = oh - mu
        var = jnp.mean(dev * dev, axis=-1, keepdims=True)
        cols = slice(hd * RET_DV, (hd + 1) * RET_DV)
        a_parts[hd] = (dev * lax.rsqrt(var + GN_EPS) * gnw_ref[:, cols] * swish).astype(BF16)

    def lru_block(blk):
        cols = slice(blk * LRU_BLOCK, (blk + 1) * LRU_BLOCK)
        xl = proj(XL_OFF + blk * LRU_BLOCK, LRU_BLOCK)
        yield
        xc = _causal_conv(xl, x_tail[:, cols], cw_ref.at[:, cols], cb_ref.at[:, cols],
                          _shift_segments)
        new_tail[blk] = xl[xl.shape[0] - x_tail.shape[0]:, :]
        yield
        ri = _dot(xc.astype(BF16), wri_ref[blk])
        yl = proj(YL_OFF + blk * LRU_BLOCK, LRU_BLOCK)
        yield
        r = _sigmoid(ri[:, :LRU_BLOCK] + br_ref[:, cols])
        i = _sigmoid(ri[:, LRU_BLOCK:] + bi_ref[:, cols])
        neg_lam = -lam_ref[:, cols]
        softplus = jnp.maximum(neg_lam, 0.0) + jnp.log1p(jnp.exp(-jnp.abs(neg_lam)))
        log_a = (-LRU_C * r) * softplus
        a = jnp.exp(log_a)
        b = jnp.sqrt(1.0 - a * a) * (i * xc)
        yield
        hl, new_h[blk] = _segment_scan(a, b, h_in[:, cols])
        b_parts[blk] = (hl * jax.nn.gelu(yl)).astype(BF16)

    chains = []
    for part in range(max(RET_HEADS, LRU_BLOCKS)):
        if part < RET_HEADS:
            chains.append(retention_head(part))
        if part < LRU_BLOCKS:
            chains.append(lru_block(part))
    while chains:
        for chain in list(chains):
            if next(chain, chain) is chain:
                chains.remove(chain)
    a_act = jnp.concatenate(a_parts, axis=1)
    b_act = jnp.concatenate(b_parts, axis=1)
    for hd in range(RET_HEADS):
        state_ref[hd] = new_states[hd]
    hcar_ref[0:1, :] = jnp.concatenate(new_h, axis=1)
    xcar_ref[...] = jnp.concatenate(new_tail, axis=1)

    mix_parts = []
    for c in range(D_MODEL // MERGE_COLS):
        cols = slice(c * MERGE_COLS, (c + 1) * MERGE_COLS)
        gate_a = _sigmoid(proj(GR_OFF + c * MERGE_COLS, MERGE_COLS) + mgb_ref[0:1, cols])
        gate_b = _sigmoid(proj(GL_OFF + c * MERGE_COLS, MERGE_COLS) + mgb_ref[1:2, cols])
        y_a = _dot(a_act, wro_ref[:, cols])
        y_b = _dot(b_act, wlo_ref[:, cols])
        mix_parts.append((gate_a * y_a + gate_b * y_b).astype(BF16))
    mix = _dot(unperm_ref[...], jnp.concatenate(mix_parts, axis=1)).astype(BF16)
    o_ref[...] = x + _dot(mix, wout_ref[...])


def _ffn_kernel(x_ref, n2w_ref, wup_ref, cw_ref, cb_ref, wdn_ref, nfw_ref, o_ref, gcar_ref, *,
                final_norm):
    @pl.when(pl.program_id(1) == 0)
    def _():
        gcar_ref[...] = jnp.zeros_like(gcar_ref)

    x = x_ref[...]
    h = _rms_norm(x, n2w_ref[...]).astype(BF16)
    acc = x
    for c in range(D_FF // FFN_COLS):
        cols = slice(c * FFN_COLS, (c + 1) * FFN_COLS)
        gate = _dot(h, wup_ref[:, cols])
        val = _dot(h, wup_ref[:, D_FF + c * FFN_COLS:D_FF + (c + 1) * FFN_COLS])
        conv = _causal_conv(gate, gcar_ref[:, cols], cw_ref.at[:, cols], cb_ref.at[:, cols],
                            _shift_rows)
        gcar_ref[:, cols] = gate[gate.shape[0] - SUBLANES:, :]
        acc = acc + _dot((jax.nn.gelu(conv) * val).astype(BF16), wdn_ref[cols, :])
    o_ref[...] = _rms_norm(acc, nfw_ref[...]) if final_norm else acc


def _const_spec(shape):
    zeros = (0,) * len(shape)
    return pl.BlockSpec(shape, lambda b, s: zeros, pipeline_mode=pl.Buffered(1))


def _segment_times(tile):
    rows = np.arange(tile)
    return (rows % SUBLANES) * (tile // SUBLANES) + rows // SUBLANES


def _retention_tables(tile, times):
    hd = np.arange(RET_HEADS, dtype=np.float64)
    log_gamma = np.log1p(-np.power(2.0, -5.0 - hd))
    idx = times.astype(np.float64)
    diff = idx[:, None] - idx[None, :]
    chunk = times // CHUNK
    same = chunk[:, None] == chunk[None, :]
    earlier = chunk[None, :] < chunk[:, None]
    expo = np.where(same, np.abs(diff), diff)
    dmask = np.where(same | earlier, np.exp(log_gamma[:, None, None] * expo), 0.0)
    q_dec = np.exp(log_gamma[None, :] * (idx[:, None] + 1.0))
    k_dec = np.exp(log_gamma[None, :] * (tile - 1.0 - idx[:, None]))
    state_decay = tuple(float(v) for v in np.exp(log_gamma * tile))
    widen = lambda t: np.repeat(t, RET_DK, axis=1).astype(np.float32)
    return dmask.astype(np.float32), widen(q_dec), widen(k_dec), state_decay


def kernel(x, positions, norm1_w, w_in, merge_gate_b, ret_gn_w, w_ret_o, lru_conv_w, lru_conv_b,
           lru_w_r, lru_b_r, lru_w_i, lru_b_i, lru_lambda, w_lru_o, w_out, norm2_w, ffn_w_up,
           ffn_conv_w, ffn_conv_b, ffn_w_down, norm_f_w):
    bsz, seq, d_model = x.shape
    depth = w_in.shape[0]
    assert d_model == D_MODEL and seq % MIX_TILE == 0 and seq % FFN_TILE == 0
    assert MIX_TILE % CHUNK == 0 and MIX_TILE // SUBLANES >= LRU_CONV - 1

    half = RET_DK // 2
    inv_freq = ROPE_BASE ** (-jnp.arange(half, dtype=F32) / half)
    inv_freq2 = jnp.concatenate([inv_freq, inv_freq])[None, :]
    times = _segment_times(MIX_TILE)
    perm = np.zeros((MIX_TILE, MIX_TILE), np.float32)
    perm[np.arange(MIX_TILE), times] = 1.0
    pos_seg = positions.reshape(bsz, seq // MIX_TILE, MIX_TILE)[:, :, times].reshape(bsz, seq, 1)
    dmask, q_dec, k_dec, state_decay = _retention_tables(MIX_TILE, times)
    row = lambda v: v.reshape(1, -1)

    params = pltpu.CompilerParams(dimension_semantics=("arbitrary", "arbitrary"),
                                  vmem_limit_bytes=VMEM_LIMIT_BYTES)
    tile_spec = lambda t: pl.BlockSpec((None, t, D_MODEL), lambda b, s: (b, s, 0))

    for l in range(depth):
        w_ri = jnp.concatenate([lru_w_r[l], lru_w_i[l]], axis=-1).astype(BF16)
        mixer_consts = [
            jnp.asarray(perm, BF16), jnp.asarray(perm.T, BF16),
            inv_freq2, row(norm1_w[l]), w_in[l].astype(BF16), merge_gate_b[l], row(ret_gn_w[l]),
            w_ret_o[l].astype(BF16), lru_conv_w[l], row(lru_conv_b[l]), w_ri,
            row(lru_b_r[l]), row(lru_b_i[l]), row(lru_lambda[l]), w_lru_o[l].astype(BF16),
            w_out[l].astype(BF16), jnp.asarray(dmask), jnp.asarray(q_dec), jnp.asarray(k_dec)]
        x = pl.pallas_call(
            functools.partial(_mixer_kernel, state_decay=state_decay),
            grid=(bsz, seq // MIX_TILE),
            in_specs=[tile_spec(MIX_TILE),
                      pl.BlockSpec((None, MIX_TILE, 1), lambda b, s: (b, s, 0))]
                     + [_const_spec(c.shape) for c in mixer_consts],
            out_specs=tile_spec(MIX_TILE),
            out_shape=jax.ShapeDtypeStruct(x.shape, x.dtype),
            scratch_shapes=[pltpu.VMEM((RET_HEADS, RET_DK, RET_DV), F32),
                            pltpu.VMEM((SUBLANES, LRU_WIDTH), F32),
                            pltpu.VMEM((SUBLANES * (LRU_CONV - 1), LRU_WIDTH), F32)],
            compiler_params=params,
            name="token_mixer",
        )(x, pos_seg, *mixer_consts)

        ffn_consts = [row(norm2_w[l]), ffn_w_up[l].astype(BF16), ffn_conv_w[l], row(ffn_conv_b[l]),
                      ffn_w_down[l].astype(BF16), row(norm_f_w)]
        x = pl.pallas_call(
            functools.partial(_ffn_kernel, final_norm=(l == depth - 1)),
            grid=(bsz, seq // FFN_TILE),
            in_specs=[tile_spec(FFN_TILE)] + [_const_spec(c.shape) for c in ffn_consts],
            out_specs=tile_spec(FFN_TILE),
            out_shape=jax.ShapeDtypeStruct(x.shape, x.dtype),
            scratch_shapes=[pltpu.VMEM((SUBLANES, D_FF), F32)],
            compiler_params=params,
            name="channel_mixer",
        )(x, *ffn_consts)
    return x
```

```python
import functools

import jax
import jax.numpy as jnp
import numpy as np
from jax import lax
from jax.experimental import pallas as pl
from jax.experimental.pallas import tpu as pltpu

D_MODEL = 1024
CHUNK = 64
RET_HEADS = 4
RET_DK = 128
RET_DV = 256
RET_QK_WIDTH = RET_HEADS * RET_DK
RET_V_WIDTH = RET_HEADS * RET_DV
LRU_WIDTH = 1024
LRU_BLOCKS = 4
LRU_BLOCK = LRU_WIDTH // LRU_BLOCKS
LRU_CONV = 4
LRU_C = 8.0
D_FF = 3 * D_MODEL
FFN_CONV = 3
ROPE_BASE = 10000.0
RMS_EPS = 1e-6
GN_EPS = 1e-6

Q_OFF = 0
K_OFF = Q_OFF + RET_QK_WIDTH
V_OFF = K_OFF + RET_QK_WIDTH
G_OFF = V_OFF + RET_V_WIDTH
XL_OFF = G_OFF + RET_V_WIDTH
YL_OFF = XL_OFF + LRU_WIDTH
GR_OFF = YL_OFF + LRU_WIDTH
GL_OFF = GR_OFF + D_MODEL
D_IN = GL_OFF + D_MODEL

SUBLANES = 8
MIX_TILE = 256
FFN_TILE = 512
FFN_COLS = 1024
VMEM_LIMIT_BYTES = 56 * 1024 * 1024

F32 = jnp.float32
BF16 = jnp.bfloat16


def _dot(a, b):
    return jnp.dot(a, b, preferred_element_type=F32)


def _sigmoid(x):
    return 1.0 / (1.0 + jnp.exp(-x))


def _rms_norm(x, w):
    ms = jnp.mean(x * x, axis=-1, keepdims=True)
    return x * lax.rsqrt(ms + RMS_EPS) * w


def _group(v, j):
    return v[j * SUBLANES:(j + 1) * SUBLANES, :]


def _sublane_index(width):
    return lax.broadcasted_iota(jnp.int32, (SUBLANES, width), 0)


def _shift_rows(cur, prev_tail, k):
    rolled = pltpu.roll(cur, k, axis=0)
    head = jnp.where(_sublane_index(cur.shape[1]) < k, pltpu.roll(prev_tail, k, axis=0),
                     rolled[:SUBLANES])
    return jnp.concatenate([head, rolled[SUBLANES:]], axis=0)


def _shift_segments(cur, prev_tail, k):
    n_rows, width = cur.shape
    n_prev = prev_tail.shape[0]
    first = _sublane_index(width) == 0
    heads = []
    for g in range(k):
        src = cur[n_rows - SUBLANES * (k - g):n_rows - SUBLANES * (k - g - 1), :]
        prev = prev_tail[n_prev - SUBLANES * (k - g):n_prev - SUBLANES * (k - g - 1), :]
        heads.append(jnp.where(first, pltpu.roll(prev, 1, axis=0), pltpu.roll(src, 1, axis=0)))
    return jnp.concatenate(heads + [cur[:n_rows - SUBLANES * k, :]], axis=0)


def _causal_conv(cur, prev_tail, w_ref, b_ref, shift):
    k_width = w_ref.shape[0]
    acc = cur * w_ref[k_width - 1:k_width, :] + b_ref[...]
    for k in range(1, k_width):
        acc = acc + shift(cur, prev_tail, k) * w_ref[k_width - 1 - k:k_width - k, :]
    return acc


def _segment_scan(a, b, h_in):
    n_groups = a.shape[0] // SUBLANES
    width = a.shape[1]
    h = jnp.zeros((SUBLANES, width), F32)
    p = jnp.ones((SUBLANES, width), F32)
    for j in range(n_groups):
        a_j = _group(a, j)
        h = a_j * h + _group(b, j)
        p = p * a_j
    sub = _sublane_index(width)
    shift = 1
    while shift < SUBLANES:
        keep = sub >= shift
        p_prev = jnp.where(keep, pltpu.roll(p, shift, axis=0), 1.0)
        h_prev = jnp.where(keep, pltpu.roll(h, shift, axis=0), 0.0)
        h = p * h_prev + h
        p = p * p_prev
        shift *= 2
    seg_end = h + p * h_in
    h = jnp.where(sub == 0, h_in, pltpu.roll(seg_end, 1, axis=0))
    out = []
    for j in range(n_groups):
        h = _group(a, j) * h + _group(b, j)
        out.append(h)
    return jnp.concatenate(out, axis=0), seg_end[SUBLANES - 1:, :]


def _rotary_tables(pos, inv_freq2):
    half_rows = pos.shape[0] // 2
    half = inv_freq2.shape[1] // 2
    lane = lax.broadcasted_iota(jnp.int32, (half_rows, inv_freq2.shape[1]), 1)
    low = lane < half
    ang = jnp.where(low, pos[:half_rows, :], pos[half_rows:, :]) * inv_freq2
    cos = jnp.cos(ang)
    sin = jnp.sin(ang)
    cos_swapped = pltpu.roll(cos, half, axis=1)
    sin_swapped = pltpu.roll(sin, half, axis=1)
    cos = jnp.concatenate([jnp.where(low, cos, cos_swapped), jnp.where(low, cos_swapped, cos)], axis=0)
    sin = jnp.concatenate([jnp.where(low, -sin, sin_swapped), jnp.where(low, -sin_swapped, sin)], axis=0)
    return cos, sin


def _mixer_kernel(x_ref, pos_ref, perm_ref, unperm_ref, invf_ref, n1w_ref, win_ref, mgb_ref,
                  gnw_ref, wro_ref, cw_ref, cb_ref, wri_ref, br_ref, bi_ref, lam_ref, wlo_ref,
                  wout_ref, dmask_ref, qdec_ref, kdec_ref, o_ref,
                  state_ref, hcar_ref, xcar_ref, *, state_decay):
    @pl.when(pl.program_id(1) == 0)
    def _():
        state_ref[...] = jnp.zeros_like(state_ref)
        hcar_ref[...] = jnp.zeros_like(hcar_ref)
        xcar_ref[...] = jnp.zeros_like(xcar_ref)

    x = x_ref[...]
    h = _rms_norm(x, n1w_ref[...]).astype(BF16)
    h = _dot(perm_ref[...], h).astype(BF16)

    def proj(off, width):
        return _dot(h, win_ref[:, off:off + width])

    cos, sin = _rotary_tables(pos_ref[...].astype(F32), invf_ref[...])
    qk = proj(Q_OFF, 2 * RET_QK_WIDTH)

    states = [state_ref[hd] for hd in range(RET_HEADS)]
    h_in = hcar_ref[0:1, :]
    x_tail = xcar_ref[...]
    new_states = [None] * RET_HEADS
    a_parts = [None] * RET_HEADS
    b_result = {}

    def retention():
        qs, ks = [], []
        for hd in range(RET_HEADS):
            qh = qk[:, hd * RET_DK:(hd + 1) * RET_DK]
            kh = qk[:, RET_QK_WIDTH + hd * RET_DK:RET_QK_WIDTH + (hd + 1) * RET_DK]
            qs.append(qh * cos + pltpu.roll(qh, RET_DK // 2, axis=1) * sin)
            ks.append((kh * cos + pltpu.roll(kh, RET_DK // 2, axis=1) * sin) * (RET_DK ** -0.5))
        v = proj(V_OFF, RET_V_WIDTH).astype(BF16)
        yield
        scores, q_dec, k_dec = [], [], []
        for hd in range(RET_HEADS):
            scores.append(lax.dot_general(qs[hd].astype(BF16), ks[hd].astype(BF16),
                                          (((1,), (1,)), ((), ())), preferred_element_type=F32))
            q_dec.append((qs[hd] * qdec_ref[:, hd * RET_DK:(hd + 1) * RET_DK]).astype(BF16))
            k_dec.append((ks[hd] * kdec_ref[:, hd * RET_DK:(hd + 1) * RET_DK]).astype(BF16))
        yield
        scores = [(scores[hd] * dmask_ref[hd]).astype(BF16) for hd in range(RET_HEADS)]
        g = proj(G_OFF, RET_V_WIDTH)
        yield
        outs = []
        for hd in range(RET_HEADS):
            vh = v[:, hd * RET_DV:(hd + 1) * RET_DV]
            outs.append(_dot(scores[hd], vh) + _dot(q_dec[hd], states[hd].astype(BF16)))
            new_states[hd] = states[hd] * state_decay[hd] + lax.dot_general(
                k_dec[hd], vh, (((0,), (0,)), ((), ())), preferred_element_type=F32)
        swish = g * _sigmoid(g)
        yield
        for hd in range(RET_HEADS):
            oh = outs[hd]
            mu = jnp.mean(oh, axis=-1, keepdims=True)
            dev = oh - mu
            var = jnp.mean(dev * dev, axis=-1, keepdims=True)
            cols = slice(hd * RET_DV, (hd + 1) * RET_DV)
            a_parts[hd] = (dev * lax.rsqrt(var + GN_EPS) * gnw_ref[:, cols] * swish[:, cols]).astype(BF16)

    def recurrence():
        xl = proj(XL_OFF, LRU_WIDTH)
        yield
        xc = _causal_conv(xl, x_tail, cw_ref, cb_ref, _shift_segments)
        b_result['tail'] = xl[xl.shape[0] - x_tail.shape[0]:, :]
        yield
        ri = [_dot(xc[:, blk * LRU_BLOCK:(blk + 1) * LRU_BLOCK].astype(BF16), wri_ref[blk])
              for blk in range(LRU_BLOCKS)]
        yl = proj(YL_OFF, LRU_WIDTH)
        yield
        r = _sigmoid(jnp.concatenate([blk[:, :LRU_BLOCK] for blk in ri], axis=1) + br_ref[...])
        i = _sigmoid(jnp.concatenate([blk[:, LRU_BLOCK:] for blk in ri], axis=1) + bi_ref[...])
        neg_lam = -lam_ref[...]
        softplus = jnp.maximum(neg_lam, 0.0) + jnp.log1p(jnp.exp(-jnp.abs(neg_lam)))
        a = jnp.exp((-LRU_C * r) * softplus)
        one_minus_a2 = 1.0 - a * a
        root = jnp.where(one_minus_a2 > 0.0, one_minus_a2 * lax.rsqrt(one_minus_a2), 0.0)
        b = root * (i * xc)
        yield
        hl, b_result['h_last'] = _segment_scan(a, b, h_in)
        b_result['act'] = (hl * jax.nn.gelu(yl)).astype(BF16)

    chains = [retention(), recurrence()]
    while chains:
        for chain in list(chains):
            if next(chain, chain) is chain:
                chains.remove(chain)
    a_act = jnp.concatenate(a_parts, axis=1)
    b_act = b_result['act']
    for hd in range(RET_HEADS):
        state_ref[hd] = new_states[hd]
    hcar_ref[0:1, :] = b_result['h_last']
    xcar_ref[...] = b_result['tail']

    gates = proj(GR_OFF, 2 * D_MODEL)
    gate_a = _sigmoid(gates[:, :D_MODEL] + mgb_ref[0:1, :])
    gate_b = _sigmoid(gates[:, D_MODEL:] + mgb_ref[1:2, :])
    mix = (gate_a * _dot(a_act, wro_ref[...]) + gate_b * _dot(b_act, wlo_ref[...])).astype(BF16)
    mix = _dot(unperm_ref[...], mix).astype(BF16)
    o_ref[...] = x + _dot(mix, wout_ref[...])


def _ffn_kernel(x_ref, n2w_ref, wup_ref, cw_ref, cb_ref, wdn_ref, nfw_ref, o_ref, gcar_ref, *,
                final_norm):
    @pl.when(pl.program_id(1) == 0)
    def _():
        gcar_ref[...] = jnp.zeros_like(gcar_ref)

    x = x_ref[...]
    h = _rms_norm(x, n2w_ref[...]).astype(BF16)
    acc = x
    for c in range(D_FF // FFN_COLS):
        cols = slice(c * FFN_COLS, (c + 1) * FFN_COLS)
        gate = _dot(h, wup_ref[:, cols])
        val = _dot(h, wup_ref[:, D_FF + c * FFN_COLS:D_FF + (c + 1) * FFN_COLS])
        conv = _causal_conv(gate, gcar_ref[:, cols], cw_ref.at[:, cols], cb_ref.at[:, cols],
                            _shift_rows)
        gcar_ref[:, cols] = gate[gate.shape[0] - SUBLANES:, :]
        acc = acc + _dot((jax.nn.gelu(conv) * val).astype(BF16), wdn_ref[cols, :])
    o_ref[...] = _rms_norm(acc, nfw_ref[...]) if final_norm else acc


def _const_spec(shape):
    zeros = (0,) * len(shape)
    return pl.BlockSpec(shape, lambda b, s: zeros, pipeline_mode=pl.Buffered(1))


def _segment_times(tile):
    rows = np.arange(tile)
    return (rows % SUBLANES) * (tile // SUBLANES) + rows // SUBLANES


def _retention_tables(tile, times):
    hd = np.arange(RET_HEADS, dtype=np.float64)
    log_gamma = np.log1p(-np.power(2.0, -5.0 - hd))
    idx = times.astype(np.float64)
    diff = idx[:, None] - idx[None, :]
    chunk = times // CHUNK
    same = chunk[:, None] == chunk[None, :]
    earlier = chunk[None, :] < chunk[:, None]
    expo = np.where(same, np.abs(diff), diff)
    dmask = np.where(same | earlier, np.exp(log_gamma[:, None, None] * expo), 0.0)
    q_dec = np.exp(log_gamma[None, :] * (idx[:, None] + 1.0))
    k_dec = np.exp(log_gamma[None, :] * (tile - 1.0 - idx[:, None]))
    state_decay = tuple(float(v) for v in np.exp(log_gamma * tile))
    widen = lambda t: np.repeat(t, RET_DK, axis=1).astype(np.float32)
    return dmask.astype(np.float32), widen(q_dec), widen(k_dec), state_decay


def kernel(x, positions, norm1_w, w_in, merge_gate_b, ret_gn_w, w_ret_o, lru_conv_w, lru_conv_b,
           lru_w_r, lru_b_r, lru_w_i, lru_b_i, lru_lambda, w_lru_o, w_out, norm2_w, ffn_w_up,
           ffn_conv_w, ffn_conv_b, ffn_w_down, norm_f_w):
    bsz, seq, d_model = x.shape
    depth = w_in.shape[0]
    assert d_model == D_MODEL and seq % MIX_TILE == 0 and seq % FFN_TILE == 0
    assert MIX_TILE % CHUNK == 0 and MIX_TILE // SUBLANES >= LRU_CONV - 1

    half = RET_DK // 2
    inv_freq = ROPE_BASE ** (-jnp.arange(half, dtype=F32) / half)
    inv_freq2 = jnp.concatenate([inv_freq, inv_freq])[None, :]
    times = _segment_times(MIX_TILE)
    perm = np.zeros((MIX_TILE, MIX_TILE), np.float32)
    perm[np.arange(MIX_TILE), times] = 1.0
    pos_seg = positions.reshape(bsz, seq // MIX_TILE, SUBLANES, MIX_TILE // SUBLANES)
    pos_seg = pos_seg.transpose(0, 1, 3, 2).reshape(bsz, seq, 1)
    dmask, q_dec, k_dec, state_decay = _retention_tables(MIX_TILE, times)
    row = lambda v: v.reshape(1, -1)

    params = pltpu.CompilerParams(dimension_semantics=("arbitrary", "arbitrary"),
                                  vmem_limit_bytes=VMEM_LIMIT_BYTES)
    tile_spec = lambda t: pl.BlockSpec((None, t, D_MODEL), lambda b, s: (b, s, 0))

    for l in range(depth):
        w_ri = jnp.concatenate([lru_w_r[l], lru_w_i[l]], axis=-1).astype(BF16)
        mixer_consts = [
            jnp.asarray(perm, BF16), jnp.asarray(perm.T, BF16),
            inv_freq2, row(norm1_w[l]), w_in[l].astype(BF16), merge_gate_b[l], row(ret_gn_w[l]),
            w_ret_o[l].astype(BF16), lru_conv_w[l], row(lru_conv_b[l]), w_ri,
            row(lru_b_r[l]), row(lru_b_i[l]), row(lru_lambda[l]), w_lru_o[l].astype(BF16),
            w_out[l].astype(BF16), jnp.asarray(dmask), jnp.asarray(q_dec), jnp.asarray(k_dec)]
        x = pl.pallas_call(
            functools.partial(_mixer_kernel, state_decay=state_decay),
            grid=(bsz, seq // MIX_TILE),
            in_specs=[tile_spec(MIX_TILE),
                      pl.BlockSpec((None, MIX_TILE, 1), lambda b, s: (b, s, 0))]
                     + [_const_spec(c.shape) for c in mixer_consts],
            out_specs=tile_spec(MIX_TILE),
            out_shape=jax.ShapeDtypeStruct(x.shape, x.dtype),
            scratch_shapes=[pltpu.VMEM((RET_HEADS, RET_DK, RET_DV), F32),
                            pltpu.VMEM((SUBLANES, LRU_WIDTH), F32),
                            pltpu.VMEM((SUBLANES * (LRU_CONV - 1), LRU_WIDTH), F32)],
            compiler_params=params,
            name="token_mixer",
        )(x, pos_seg, *mixer_consts)

        ffn_consts = [row(norm2_w[l]), ffn_w_up[l].astype(BF16), ffn_conv_w[l], row(ffn_conv_b[l]),
                      ffn_w_down[l].astype(BF16), row(norm_f_w)]
        x = pl.pallas_call(
            functools.partial(_ffn_kernel, final_norm=(l == depth - 1)),
            grid=(bsz, seq // FFN_TILE),
            in_specs=[tile_spec(FFN_TILE)] + [_const_spec(c.shape) for c in ffn_consts],
            out_specs=tile_spec(FFN_TILE),
            out_shape=jax.ShapeDtypeStruct(x.shape, x.dtype),
            scratch_shapes=[pltpu.VMEM((SUBLANES, D_FF), F32)],
            compiler_params=params,
            name="channel_mixer",
        )(x, *ffn_consts)
    return x
```

```python
import functools

import jax
import jax.numpy as jnp
import numpy as np
from jax import lax
from jax.experimental import pallas as pl
from jax.experimental.pallas import tpu as pltpu

D_MODEL = 1024
CHUNK = 64
RET_HEADS = 4
RET_DK = 128
RET_DV = 256
RET_QK_WIDTH = RET_HEADS * RET_DK
RET_V_WIDTH = RET_HEADS * RET_DV
LRU_WIDTH = 1024
LRU_BLOCKS = 4
LRU_BLOCK = LRU_WIDTH // LRU_BLOCKS
LRU_CONV = 4
LRU_C = 8.0
D_FF = 3 * D_MODEL
FFN_CONV = 3
ROPE_BASE = 10000.0
RMS_EPS = 1e-6
GN_EPS = 1e-6

Q_OFF = 0
K_OFF = Q_OFF + RET_QK_WIDTH
V_OFF = K_OFF + RET_QK_WIDTH
G_OFF = V_OFF + RET_V_WIDTH
XL_OFF = G_OFF + RET_V_WIDTH
YL_OFF = XL_OFF + LRU_WIDTH
GR_OFF = YL_OFF + LRU_WIDTH
GL_OFF = GR_OFF + D_MODEL
D_IN = GL_OFF + D_MODEL

SUBLANES = 8
MIX_SUB = 256
MIX_TILE = 512
FFN_TILE = 1024
FFN_COLS = 1024
VMEM_LIMIT_BYTES = 56 * 1024 * 1024

F32 = jnp.float32
BF16 = jnp.bfloat16


def _dot(a, b):
    return jnp.dot(a, b, preferred_element_type=F32)


def _sigmoid(x):
    return 1.0 / (1.0 + jnp.exp(-x))


def _rms_norm(x, w):
    ms = jnp.mean(x * x, axis=-1, keepdims=True)
    return x * lax.rsqrt(ms + RMS_EPS) * w


def _group(v, j):
    return v[j * SUBLANES:(j + 1) * SUBLANES, :]


def _sublane_index(width):
    return lax.broadcasted_iota(jnp.int32, (SUBLANES, width), 0)


def _shift_rows(cur, prev_tail, k):
    rolled = pltpu.roll(cur, k, axis=0)
    head = jnp.where(_sublane_index(cur.shape[1]) < k, pltpu.roll(prev_tail, k, axis=0),
                     rolled[:SUBLANES])
    return jnp.concatenate([head, rolled[SUBLANES:]], axis=0)


def _shift_segments(cur, prev_tail, k):
    n_rows, width = cur.shape
    n_prev = prev_tail.shape[0]
    first = _sublane_index(width) == 0
    heads = []
    for g in range(k):
        src = cur[n_rows - SUBLANES * (k - g):n_rows - SUBLANES * (k - g - 1), :]
        prev = prev_tail[n_prev - SUBLANES * (k - g):n_prev - SUBLANES * (k - g - 1), :]
        heads.append(jnp.where(first, pltpu.roll(prev, 1, axis=0), pltpu.roll(src, 1, axis=0)))
    return jnp.concatenate(heads + [cur[:n_rows - SUBLANES * k, :]], axis=0)


def _causal_conv(cur, prev_tail, w_ref, b_ref, shift):
    k_width = w_ref.shape[0]
    acc = cur * w_ref[k_width - 1:k_width, :] + b_ref[...]
    for k in range(1, k_width):
        acc = acc + shift(cur, prev_tail, k) * w_ref[k_width - 1 - k:k_width - k, :]
    return acc


def _segment_scan(a, b, h_in):
    n_groups = a.shape[0] // SUBLANES
    width = a.shape[1]
    h = jnp.zeros((SUBLANES, width), F32)
    p = jnp.ones((SUBLANES, width), F32)
    for j in range(n_groups):
        a_j = _group(a, j)
        h = a_j * h + _group(b, j)
        p = p * a_j
    sub = _sublane_index(width)
    shift = 1
    while shift < SUBLANES:
        keep = sub >= shift
        p_prev = jnp.where(keep, pltpu.roll(p, shift, axis=0), 1.0)
        h_prev = jnp.where(keep, pltpu.roll(h, shift, axis=0), 0.0)
        h = p * h_prev + h
        p = p * p_prev
        shift *= 2
    seg_end = h + p * h_in
    h = jnp.where(sub == 0, h_in, pltpu.roll(seg_end, 1, axis=0))
    out = []
    for j in range(n_groups):
        h = _group(a, j) * h + _group(b, j)
        out.append(h)
    return jnp.concatenate(out, axis=0), seg_end[SUBLANES - 1:, :]


def _rotary_tables(pos, inv_freq2):
    half_rows = pos.shape[0] // 2
    half = inv_freq2.shape[1] // 2
    lane = lax.broadcasted_iota(jnp.int32, (half_rows, inv_freq2.shape[1]), 1)
    low = lane < half
    ang = jnp.where(low, pos[:half_rows, :], pos[half_rows:, :]) * inv_freq2
    cos = jnp.cos(ang)
    sin = jnp.sin(ang)
    cos_swapped = pltpu.roll(cos, half, axis=1)
    sin_swapped = pltpu.roll(sin, half, axis=1)
    cos = jnp.concatenate([jnp.where(low, cos, cos_swapped), jnp.where(low, cos_swapped, cos)], axis=0)
    sin = jnp.concatenate([jnp.where(low, -sin, sin_swapped), jnp.where(low, -sin_swapped, sin)], axis=0)
    return cos, sin


def _mixer_kernel(x_ref, pos_ref, perm_ref, unperm_ref, invf_ref, n1w_ref, win_ref, mgb_ref,
                  gnw_ref, wro_ref, cw_ref, cb_ref, wri_ref, br_ref, bi_ref, lam_ref, wlo_ref,
                  wout_ref, dmask_ref, qdec_ref, kdec_ref, o_ref,
                  state_ref, hcar_ref, xcar_ref, *, state_decay):
    @pl.when(pl.program_id(1) == 0)
    def _():
        state_ref[...] = jnp.zeros_like(state_ref)
        hcar_ref[...] = jnp.zeros_like(hcar_ref)
        xcar_ref[...] = jnp.zeros_like(xcar_ref)

    subs = [slice(u * MIX_SUB, (u + 1) * MIX_SUB) for u in range(x_ref.shape[0] // MIX_SUB)]
    x = x_ref[...]
    h = _rms_norm(x, n1w_ref[...]).astype(BF16)
    h = jnp.concatenate([_dot(perm_ref[...], h[rows, :]).astype(BF16) for rows in subs], axis=0)

    def proj(off, width):
        return _dot(h, win_ref[:, off:off + width])

    pos = pos_ref[...].astype(F32)
    tables = [_rotary_tables(pos[rows, :], invf_ref[...]) for rows in subs]
    cos = jnp.concatenate([t[0] for t in tables], axis=0)
    sin = jnp.concatenate([t[1] for t in tables], axis=0)
    qk = proj(Q_OFF, 2 * RET_QK_WIDTH)

    states = [state_ref[hd] for hd in range(RET_HEADS)]
    h_in = hcar_ref[0:1, :]
    x_tail = xcar_ref[...]
    new_states = [None] * RET_HEADS
    a_parts = [None] * RET_HEADS
    b_result = {}

    def retention():
        qs, ks = [], []
        for hd in range(RET_HEADS):
            qh = qk[:, hd * RET_DK:(hd + 1) * RET_DK]
            kh = qk[:, RET_QK_WIDTH + hd * RET_DK:RET_QK_WIDTH + (hd + 1) * RET_DK]
            qs.append(qh * cos + pltpu.roll(qh, RET_DK // 2, axis=1) * sin)
            ks.append((kh * cos + pltpu.roll(kh, RET_DK // 2, axis=1) * sin) * (RET_DK ** -0.5))
        v = proj(V_OFF, RET_V_WIDTH).astype(BF16)
        yield
        scores, q_dec, k_dec = [], [], []
        for hd in range(RET_HEADS):
            qb = qs[hd].astype(BF16)
            kb = ks[hd].astype(BF16)
            dk_cols = slice(hd * RET_DK, (hd + 1) * RET_DK)
            scores.append([lax.dot_general(qb[rows, :], kb[rows, :], (((1,), (1,)), ((), ())),
                                           preferred_element_type=F32) for rows in subs])
            q_dec.append([(qs[hd][rows, :] * qdec_ref[:, dk_cols]).astype(BF16) for rows in subs])
            k_dec.append([(ks[hd][rows, :] * kdec_ref[:, dk_cols]).astype(BF16) for rows in subs])
        yield
        scores = [[(sc * dmask_ref[hd]).astype(BF16) for sc in scores[hd]] for hd in range(RET_HEADS)]
        g = proj(G_OFF, RET_V_WIDTH)
        yield
        outs = []
        for hd in range(RET_HEADS):
            vh = v[:, hd * RET_DV:(hd + 1) * RET_DV]
            state = states[hd]
            parts = []
            for u, rows in enumerate(subs):
                parts.append(_dot(scores[hd][u], vh[rows, :]) + _dot(q_dec[hd][u], state.astype(BF16)))
                state = state * state_decay[hd] + lax.dot_general(
                    k_dec[hd][u], vh[rows, :], (((0,), (0,)), ((), ())), preferred_element_type=F32)
            new_states[hd] = state
            outs.append(jnp.concatenate(parts, axis=0))
        swish = g * _sigmoid(g)
        yield
        for hd in range(RET_HEADS):
            oh = outs[hd]
            mu = jnp.mean(oh, axis=-1, keepdims=True)
            dev = oh - mu
            var = jnp.mean(dev * dev, axis=-1, keepdims=True)
            cols = slice(hd * RET_DV, (hd + 1) * RET_DV)
            a_parts[hd] = (dev * lax.rsqrt(var + GN_EPS) * gnw_ref[:, cols] * swish[:, cols]).astype(BF16)

    def recurrence():
        xl = proj(XL_OFF, LRU_WIDTH)
        yield
        tail = x_tail
        xc = []
        for rows in subs:
            cur = xl[rows, :]
            xc.append(_causal_conv(cur, tail, cw_ref, cb_ref, _shift_segments))
            tail = cur[MIX_SUB - x_tail.shape[0]:, :]
        b_result['tail'] = tail
        xc = jnp.concatenate(xc, axis=0)
        yield
        ri = [_dot(xc[:, blk * LRU_BLOCK:(blk + 1) * LRU_BLOCK].astype(BF16), wri_ref[blk])
              for blk in range(LRU_BLOCKS)]
        yl = proj(YL_OFF, LRU_WIDTH)
        yield
        r = _sigmoid(jnp.concatenate([blk[:, :LRU_BLOCK] for blk in ri], axis=1) + br_ref[...])
        i = _sigmoid(jnp.concatenate([blk[:, LRU_BLOCK:] for blk in ri], axis=1) + bi_ref[...])
        neg_lam = -lam_ref[...]
        softplus = jnp.maximum(neg_lam, 0.0) + jnp.log1p(jnp.exp(-jnp.abs(neg_lam)))
        a = jnp.exp((-LRU_C * r) * softplus)
        one_minus_a2 = 1.0 - a * a
        root = jnp.where(one_minus_a2 > 0.0, one_minus_a2 * lax.rsqrt(one_minus_a2), 0.0)
        b = root * (i * xc)
        yield
        carry = h_in
        hl = []
        for rows in subs:
            h_sub, carry = _segment_scan(a[rows, :], b[rows, :], carry)
            hl.append(h_sub)
        b_result['h_last'] = carry
        b_result['act'] = (jnp.concatenate(hl, axis=0) * jax.nn.gelu(yl)).astype(BF16)

    chains = [retention(), recurrence()]
    while chains:
        for chain in list(chains):
            if next(chain, chain) is chain:
                chains.remove(chain)
    a_act = jnp.concatenate(a_parts, axis=1)
    b_act = b_result['act']
    for hd in range(RET_HEADS):
        state_ref[hd] = new_states[hd]
    hcar_ref[0:1, :] = b_result['h_last']
    xcar_ref[...] = b_result['tail']

    gates = proj(GR_OFF, 2 * D_MODEL)
    gate_a = _sigmoid(gates[:, :D_MODEL] + mgb_ref[0:1, :])
    gate_b = _sigmoid(gates[:, D_MODEL:] + mgb_ref[1:2, :])
    mix = (gate_a * _dot(a_act, wro_ref[...]) + gate_b * _dot(b_act, wlo_ref[...])).astype(BF16)
    mix = jnp.concatenate([_dot(unperm_ref[...], mix[rows, :]).astype(BF16) for rows in subs], axis=0)
    o_ref[...] = x + _dot(mix, wout_ref[...])


def _ffn_kernel(x_ref, n2w_ref, wup_ref, cw_ref, cb_ref, wdn_ref, nfw_ref, o_ref, gcar_ref, *,
                final_norm):
    @pl.when(pl.program_id(1) == 0)
    def _():
        gcar_ref[...] = jnp.zeros_like(gcar_ref)

    x = x_ref[...]
    h = _rms_norm(x, n2w_ref[...]).astype(BF16)
    acc = x
    for c in range(D_FF // FFN_COLS):
        cols = slice(c * FFN_COLS, (c + 1) * FFN_COLS)
        gate = _dot(h, wup_ref[:, cols])
        val = _dot(h, wup_ref[:, D_FF + c * FFN_COLS:D_FF + (c + 1) * FFN_COLS])
        conv = _causal_conv(gate, gcar_ref[:, cols], cw_ref.at[:, cols], cb_ref.at[:, cols],
                            _shift_rows)
        gcar_ref[:, cols] = gate[gate.shape[0] - SUBLANES:, :]
        acc = acc + _dot((jax.nn.gelu(conv) * val).astype(BF16), wdn_ref[cols, :])
    o_ref[...] = _rms_norm(acc, nfw_ref[...]) if final_norm else acc


def _const_spec(shape):
    zeros = (0,) * len(shape)
    return pl.BlockSpec(shape, lambda b, s: zeros, pipeline_mode=pl.Buffered(1))


def _segment_times(tile):
    rows = np.arange(tile)
    return (rows % SUBLANES) * (tile // SUBLANES) + rows // SUBLANES


def _retention_tables(tile, times):
    hd = np.arange(RET_HEADS, dtype=np.float64)
    log_gamma = np.log1p(-np.power(2.0, -5.0 - hd))
    idx = times.astype(np.float64)
    diff = idx[:, None] - idx[None, :]
    chunk = times // CHUNK
    same = chunk[:, None] == chunk[None, :]
    earlier = chunk[None, :] < chunk[:, None]
    expo = np.where(same, np.abs(diff), diff)
    dmask = np.where(same | earlier, np.exp(log_gamma[:, None, None] * expo), 0.0)
    q_dec = np.exp(log_gamma[None, :] * (idx[:, None] + 1.0))
    k_dec = np.exp(log_gamma[None, :] * (tile - 1.0 - idx[:, None]))
    state_decay = tuple(float(v) for v in np.exp(log_gamma * tile))
    widen = lambda t: np.repeat(t, RET_DK, axis=1).astype(np.float32)
    return dmask.astype(np.float32), widen(q_dec), widen(k_dec), state_decay


def kernel(x, positions, norm1_w, w_in, merge_gate_b, ret_gn_w, w_ret_o, lru_conv_w, lru_conv_b,
           lru_w_r, lru_b_r, lru_w_i, lru_b_i, lru_lambda, w_lru_o, w_out, norm2_w, ffn_w_up,
           ffn_conv_w, ffn_conv_b, ffn_w_down, norm_f_w):
    bsz, seq, d_model = x.shape
    depth = w_in.shape[0]
    assert d_model == D_MODEL and seq % MIX_TILE == 0 and seq % FFN_TILE == 0
    assert MIX_TILE % MIX_SUB == 0 and MIX_SUB % CHUNK == 0 and MIX_SUB // SUBLANES >= LRU_CONV - 1

    half = RET_DK // 2
    inv_freq = ROPE_BASE ** (-jnp.arange(half, dtype=F32) / half)
    inv_freq2 = jnp.concatenate([inv_freq, inv_freq])[None, :]
    times = _segment_times(MIX_SUB)
    perm = np.zeros((MIX_SUB, MIX_SUB), np.float32)
    perm[np.arange(MIX_SUB), times] = 1.0
    pos_seg = positions.reshape(bsz, seq // MIX_SUB, SUBLANES, MIX_SUB // SUBLANES)
    pos_seg = pos_seg.transpose(0, 1, 3, 2).reshape(bsz, seq, 1)
    dmask, q_dec, k_dec, state_decay = _retention_tables(MIX_SUB, times)
    row = lambda v: v.reshape(1, -1)

    params = pltpu.CompilerParams(dimension_semantics=("arbitrary", "arbitrary"),
                                  vmem_limit_bytes=VMEM_LIMIT_BYTES)
    tile_spec = lambda t: pl.BlockSpec((None, t, D_MODEL), lambda b, s: (b, s, 0))

    for l in range(depth):
        w_ri = jnp.concatenate([lru_w_r[l], lru_w_i[l]], axis=-1).astype(BF16)
        mixer_consts = [
            jnp.asarray(perm, BF16), jnp.asarray(perm.T, BF16),
            inv_freq2, row(norm1_w[l]), w_in[l].astype(BF16), merge_gate_b[l], row(ret_gn_w[l]),
            w_ret_o[l].astype(BF16), lru_conv_w[l], row(lru_conv_b[l]), w_ri,
            row(lru_b_r[l]), row(lru_b_i[l]), row(lru_lambda[l]), w_lru_o[l].astype(BF16),
            w_out[l].astype(BF16), jnp.asarray(dmask), jnp.asarray(q_dec), jnp.asarray(k_dec)]
        x = pl.pallas_call(
            functools.partial(_mixer_kernel, state_decay=state_decay),
            grid=(bsz, seq // MIX_TILE),
            in_specs=[tile_spec(MIX_TILE),
                      pl.BlockSpec((None, MIX_TILE, 1), lambda b, s: (b, s, 0))]
                     + [_const_spec(c.shape) for c in mixer_consts],
            out_specs=tile_spec(MIX_TILE),
            out_shape=jax.ShapeDtypeStruct(x.shape, x.dtype),
            scratch_shapes=[pltpu.VMEM((RET_HEADS, RET_DK, RET_DV), F32),
                            pltpu.VMEM((SUBLANES, LRU_WIDTH), F32),
                            pltpu.VMEM((SUBLANES * (LRU_CONV - 1), LRU_WIDTH), F32)],
            compiler_params=params,
            name="token_mixer",
        )(x, pos_seg, *mixer_consts)

        ffn_consts = [row(norm2_w[l]), ffn_w_up[l].astype(BF16), ffn_conv_w[l], row(ffn_conv_b[l]),
                      ffn_w_down[l].astype(BF16), row(norm_f_w)]
        x = pl.pallas_call(
            functools.partial(_ffn_kernel, final_norm=(l == depth - 1)),
            grid=(bsz, seq // FFN_TILE),
            in_specs=[tile_spec(FFN_TILE)] + [_const_spec(c.shape) for c in ffn_consts],
            out_specs=tile_spec(FFN_TILE),
            out_shape=jax.ShapeDtypeStruct(x.shape, x.dtype),
            scratch_shapes=[pltpu.VMEM((SUBLANES, D_FF), F32)],
            compiler_params=params,
            name="channel_mixer",
        )(x, *ffn_consts)
    return x
```

```python
import functools

import jax
import jax.numpy as jnp
import numpy as np
from jax import lax
from jax.experimental import pallas as pl
from jax.experimental.pallas import tpu as pltpu

D_MODEL = 1024
CHUNK = 64
RET_HEADS = 4
RET_DK = 128
RET_DV = 256
RET_QK_WIDTH = RET_HEADS * RET_DK
RET_V_WIDTH = RET_HEADS * RET_DV
LRU_WIDTH = 1024
LRU_BLOCKS = 4
LRU_BLOCK = LRU_WIDTH // LRU_BLOCKS
LRU_CONV = 4
LRU_C = 8.0
D_FF = 3 * D_MODEL
FFN_CONV = 3
ROPE_BASE = 10000.0
RMS_EPS = 1e-6
GN_EPS = 1e-6

Q_OFF = 0
K_OFF = Q_OFF + RET_QK_WIDTH
V_OFF = K_OFF + RET_QK_WIDTH
G_OFF = V_OFF + RET_V_WIDTH
XL_OFF = G_OFF + RET_V_WIDTH
YL_OFF = XL_OFF + LRU_WIDTH
GR_OFF = YL_OFF + LRU_WIDTH
GL_OFF = GR_OFF + D_MODEL
D_IN = GL_OFF + D_MODEL

SUBLANES = 8
MIX_SUB = 256
MIX_TILE = 512
FFN_TILE = 1024
FFN_COLS = D_FF
VMEM_LIMIT_BYTES = 56 * 1024 * 1024

F32 = jnp.float32
BF16 = jnp.bfloat16


def _dot(a, b):
    return jnp.dot(a, b, preferred_element_type=F32)


def _sigmoid(x):
    return 1.0 / (1.0 + jnp.exp(-x))


def _rms_norm(x, w):
    ms = jnp.mean(x * x, axis=-1, keepdims=True)
    return x * lax.rsqrt(ms + RMS_EPS) * w


def _group(v, j):
    return v[j * SUBLANES:(j + 1) * SUBLANES, :]


def _sublane_index(width):
    return lax.broadcasted_iota(jnp.int32, (SUBLANES, width), 0)


def _shift_rows(cur, prev_tail, k):
    rolled = pltpu.roll(cur, k, axis=0)
    head = jnp.where(_sublane_index(cur.shape[1]) < k, pltpu.roll(prev_tail, k, axis=0),
                     rolled[:SUBLANES])
    return jnp.concatenate([head, rolled[SUBLANES:]], axis=0)


def _shift_segments(cur, prev_tail, k):
    n_rows, width = cur.shape
    n_prev = prev_tail.shape[0]
    first = _sublane_index(width) == 0
    heads = []
    for g in range(k):
        src = cur[n_rows - SUBLANES * (k - g):n_rows - SUBLANES * (k - g - 1), :]
        prev = prev_tail[n_prev - SUBLANES * (k - g):n_prev - SUBLANES * (k - g - 1), :]
        heads.append(jnp.where(first, pltpu.roll(prev, 1, axis=0), pltpu.roll(src, 1, axis=0)))
    return jnp.concatenate(heads + [cur[:n_rows - SUBLANES * k, :]], axis=0)


def _causal_conv(cur, prev_tail, w_ref, b_ref, shift):
    k_width = w_ref.shape[0]
    acc = cur * w_ref[k_width - 1:k_width, :] + b_ref[...]
    for k in range(1, k_width):
        acc = acc + shift(cur, prev_tail, k) * w_ref[k_width - 1 - k:k_width - k, :]
    return acc


def _segment_scan(a, b, h_in):
    n_groups = a.shape[0] // SUBLANES
    width = a.shape[1]
    h = jnp.zeros((SUBLANES, width), F32)
    p = jnp.ones((SUBLANES, width), F32)
    for j in range(n_groups):
        a_j = _group(a, j)
        h = a_j * h + _group(b, j)
        p = p * a_j
    sub = _sublane_index(width)
    shift = 1
    while shift < SUBLANES:
        keep = sub >= shift
        p_prev = jnp.where(keep, pltpu.roll(p, shift, axis=0), 1.0)
        h_prev = jnp.where(keep, pltpu.roll(h, shift, axis=0), 0.0)
        h = p * h_prev + h
        p = p * p_prev
        shift *= 2
    seg_end = h + p * h_in
    h = jnp.where(sub == 0, h_in, pltpu.roll(seg_end, 1, axis=0))
    out = []
    for j in range(n_groups):
        h = _group(a, j) * h + _group(b, j)
        out.append(h)
    return jnp.concatenate(out, axis=0), seg_end[SUBLANES - 1:, :]


def _rotary_tables(pos, inv_freq2):
    half_rows = pos.shape[0] // 2
    half = inv_freq2.shape[1] // 2
    lane = lax.broadcasted_iota(jnp.int32, (half_rows, inv_freq2.shape[1]), 1)
    low = lane < half
    ang = jnp.where(low, pos[:half_rows, :], pos[half_rows:, :]) * inv_freq2
    cos = jnp.cos(ang)
    sin = jnp.sin(ang)
    cos_swapped = pltpu.roll(cos, half, axis=1)
    sin_swapped = pltpu.roll(sin, half, axis=1)
    cos = jnp.concatenate([jnp.where(low, cos, cos_swapped), jnp.where(low, cos_swapped, cos)], axis=0)
    sin = jnp.concatenate([jnp.where(low, -sin, sin_swapped), jnp.where(low, -sin_swapped, sin)], axis=0)
    return cos, sin


def _mixer_kernel(x_ref, pos_ref, perm_ref, unperm_ref, invf_ref, n1w_ref, win_ref, mgb_ref,
                  gnw_ref, wro_ref, cw_ref, cb_ref, wri_ref, br_ref, bi_ref, lam_ref, wlo_ref,
                  wout_ref, dmask_ref, qdec_ref, kdec_ref, o_ref,
                  state_ref, hcar_ref, xcar_ref, *, state_decay):
    @pl.when(pl.program_id(1) == 0)
    def _():
        state_ref[...] = jnp.zeros_like(state_ref)
        hcar_ref[...] = jnp.zeros_like(hcar_ref)
        xcar_ref[...] = jnp.zeros_like(xcar_ref)

    subs = [slice(u * MIX_SUB, (u + 1) * MIX_SUB) for u in range(x_ref.shape[0] // MIX_SUB)]
    x = x_ref[...]
    h = _rms_norm(x, n1w_ref[...]).astype(BF16)
    h = jnp.concatenate([_dot(perm_ref[...], h[rows, :]).astype(BF16) for rows in subs], axis=0)

    def proj(off, width):
        return _dot(h, win_ref[:, off:off + width])

    pos = pos_ref[...].astype(F32)
    tables = [_rotary_tables(pos[rows, :], invf_ref[...]) for rows in subs]
    cos = jnp.concatenate([t[0] for t in tables], axis=0)
    sin = jnp.concatenate([t[1] for t in tables], axis=0)
    qk = proj(Q_OFF, 2 * RET_QK_WIDTH)

    states = [state_ref[hd] for hd in range(RET_HEADS)]
    h_in = hcar_ref[0:1, :]
    x_tail = xcar_ref[...]
    new_states = [None] * RET_HEADS
    a_parts = [None] * RET_HEADS
    b_result = {}

    def retention():
        qs, ks = [], []
        for hd in range(RET_HEADS):
            qh = qk[:, hd * RET_DK:(hd + 1) * RET_DK]
            kh = qk[:, RET_QK_WIDTH + hd * RET_DK:RET_QK_WIDTH + (hd + 1) * RET_DK]
            qs.append(qh * cos + pltpu.roll(qh, RET_DK // 2, axis=1) * sin)
            ks.append((kh * cos + pltpu.roll(kh, RET_DK // 2, axis=1) * sin) * (RET_DK ** -0.5))
        v = proj(V_OFF, RET_V_WIDTH).astype(BF16)
        yield
        scores, q_dec, k_dec = [], [], []
        for hd in range(RET_HEADS):
            qb = qs[hd].astype(BF16)
            kb = ks[hd].astype(BF16)
            dk_cols = slice(hd * RET_DK, (hd + 1) * RET_DK)
            scores.append([lax.dot_general(qb[rows, :], kb[rows, :], (((1,), (1,)), ((), ())),
                                           preferred_element_type=F32) for rows in subs])
            q_dec.append([(qs[hd][rows, :] * qdec_ref[:, dk_cols]).astype(BF16) for rows in subs])
            k_dec.append([(ks[hd][rows, :] * kdec_ref[:, dk_cols]).astype(BF16) for rows in subs])
        yield
        scores = [[(sc * dmask_ref[hd]).astype(BF16) for sc in scores[hd]] for hd in range(RET_HEADS)]
        g = proj(G_OFF, RET_V_WIDTH)
        yield
        outs = []
        for hd in range(RET_HEADS):
            vh = v[:, hd * RET_DV:(hd + 1) * RET_DV]
            state = states[hd]
            parts = []
            for u, rows in enumerate(subs):
                parts.append(_dot(scores[hd][u], vh[rows, :]) + _dot(q_dec[hd][u], state.astype(BF16)))
                state = state * state_decay[hd] + lax.dot_general(
                    k_dec[hd][u], vh[rows, :], (((0,), (0,)), ((), ())), preferred_element_type=F32)
            new_states[hd] = state
            outs.append(jnp.concatenate(parts, axis=0))
        swish = g * _sigmoid(g)
        yield
        for hd in range(RET_HEADS):
            oh = outs[hd]
            mu = jnp.mean(oh, axis=-1, keepdims=True)
            dev = oh - mu
            var = jnp.mean(dev * dev, axis=-1, keepdims=True)
            cols = slice(hd * RET_DV, (hd + 1) * RET_DV)
            a_parts[hd] = (dev * lax.rsqrt(var + GN_EPS) * gnw_ref[:, cols] * swish[:, cols]).astype(BF16)

    def recurrence():
        xl = proj(XL_OFF, LRU_WIDTH)
        yield
        tail = x_tail
        xc = []
        for rows in subs:
            cur = xl[rows, :]
            xc.append(_causal_conv(cur, tail, cw_ref, cb_ref, _shift_segments))
            tail = cur[MIX_SUB - x_tail.shape[0]:, :]
        b_result['tail'] = tail
        xc = jnp.concatenate(xc, axis=0)
        yield
        ri = [_dot(xc[:, blk * LRU_BLOCK:(blk + 1) * LRU_BLOCK].astype(BF16), wri_ref[blk])
              for blk in range(LRU_BLOCKS)]
        yl = proj(YL_OFF, LRU_WIDTH)
        yield
        r = _sigmoid(jnp.concatenate([blk[:, :LRU_BLOCK] for blk in ri], axis=1) + br_ref[...])
        i = _sigmoid(jnp.concatenate([blk[:, LRU_BLOCK:] for blk in ri], axis=1) + bi_ref[...])
        neg_lam = -lam_ref[...]
        softplus = jnp.maximum(neg_lam, 0.0) + jnp.log1p(jnp.exp(-jnp.abs(neg_lam)))
        a = jnp.exp((-LRU_C * r) * softplus)
        one_minus_a2 = 1.0 - a * a
        root = jnp.where(one_minus_a2 > 0.0, one_minus_a2 * lax.rsqrt(one_minus_a2), 0.0)
        b = root * (i * xc)
        yield
        carry = h_in
        hl = []
        for rows in subs:
            h_sub, carry = _segment_scan(a[rows, :], b[rows, :], carry)
            hl.append(h_sub)
        b_result['h_last'] = carry
        b_result['act'] = (jnp.concatenate(hl, axis=0) * jax.nn.gelu(yl)).astype(BF16)

    chains = [retention(), recurrence()]
    while chains:
        for chain in list(chains):
            if next(chain, chain) is chain:
                chains.remove(chain)
    a_act = jnp.concatenate(a_parts, axis=1)
    b_act = b_result['act']
    for hd in range(RET_HEADS):
        state_ref[hd] = new_states[hd]
    hcar_ref[0:1, :] = b_result['h_last']
    xcar_ref[...] = b_result['tail']

    gates = proj(GR_OFF, 2 * D_MODEL)
    gate_a = _sigmoid(gates[:, :D_MODEL] + mgb_ref[0:1, :])
    gate_b = _sigmoid(gates[:, D_MODEL:] + mgb_ref[1:2, :])
    mix = (gate_a * _dot(a_act, wro_ref[...]) + gate_b * _dot(b_act, wlo_ref[...])).astype(BF16)
    mix = jnp.concatenate([_dot(unperm_ref[...], mix[rows, :]).astype(BF16) for rows in subs], axis=0)
    o_ref[...] = x + _dot(mix, wout_ref[...])


def _ffn_kernel(x_ref, n2w_ref, wup_ref, cw_ref, cb_ref, wdn_ref, nfw_ref, o_ref, gcar_ref, *,
                final_norm):
    @pl.when(pl.program_id(1) == 0)
    def _():
        gcar_ref[...] = jnp.zeros_like(gcar_ref)

    x = x_ref[...]
    h = _rms_norm(x, n2w_ref[...]).astype(BF16)
    acc = x
    for c in range(D_FF // FFN_COLS):
        cols = slice(c * FFN_COLS, (c + 1) * FFN_COLS)
        gate = _dot(h, wup_ref[:, cols])
        val = _dot(h, wup_ref[:, D_FF + c * FFN_COLS:D_FF + (c + 1) * FFN_COLS])
        conv = _causal_conv(gate, gcar_ref[:, cols], cw_ref.at[:, cols], cb_ref.at[:, cols],
                            _shift_rows)
        gcar_ref[:, cols] = gate[gate.shape[0] - SUBLANES:, :]
        acc = acc + _dot((jax.nn.gelu(conv) * val).astype(BF16), wdn_ref[cols, :])
    o_ref[...] = _rms_norm(acc, nfw_ref[...]) if final_norm else acc


def _const_spec(shape):
    zeros = (0,) * len(shape)
    return pl.BlockSpec(shape, lambda b, s: zeros, pipeline_mode=pl.Buffered(1))


def _segment_times(tile):
    rows = np.arange(tile)
    return (rows % SUBLANES) * (tile // SUBLANES) + rows // SUBLANES


def _retention_tables(tile, times):
    hd = np.arange(RET_HEADS, dtype=np.float64)
    log_gamma = np.log1p(-np.power(2.0, -5.0 - hd))
    idx = times.astype(np.float64)
    diff = idx[:, None] - idx[None, :]
    chunk = times // CHUNK
    same = chunk[:, None] == chunk[None, :]
    earlier = chunk[None, :] < chunk[:, None]
    expo = np.where(same, np.abs(diff), diff)
    dmask = np.where(same | earlier, np.exp(log_gamma[:, None, None] * expo), 0.0)
    q_dec = np.exp(log_gamma[None, :] * (idx[:, None] + 1.0))
    k_dec = np.exp(log_gamma[None, :] * (tile - 1.0 - idx[:, None]))
    state_decay = tuple(float(v) for v in np.exp(log_gamma * tile))
    widen = lambda t: np.repeat(t, RET_DK, axis=1).astype(np.float32)
    return dmask.astype(np.float32), widen(q_dec), widen(k_dec), state_decay


def kernel(x, positions, norm1_w, w_in, merge_gate_b, ret_gn_w, w_ret_o, lru_conv_w, lru_conv_b,
           lru_w_r, lru_b_r, lru_w_i, lru_b_i, lru_lambda, w_lru_o, w_out, norm2_w, ffn_w_up,
           ffn_conv_w, ffn_conv_b, ffn_w_down, norm_f_w):
    bsz, seq, d_model = x.shape
    depth = w_in.shape[0]
    assert d_model == D_MODEL and seq % MIX_TILE == 0 and seq % FFN_TILE == 0
    assert MIX_TILE % MIX_SUB == 0 and MIX_SUB % CHUNK == 0 and MIX_SUB // SUBLANES >= LRU_CONV - 1

    half = RET_DK // 2
    inv_freq = ROPE_BASE ** (-jnp.arange(half, dtype=F32) / half)
    inv_freq2 = jnp.concatenate([inv_freq, inv_freq])[None, :]
    times = _segment_times(MIX_SUB)
    perm = np.zeros((MIX_SUB, MIX_SUB), np.float32)
    perm[np.arange(MIX_SUB), times] = 1.0
    pos_seg = positions.reshape(bsz, seq // MIX_SUB, SUBLANES, MIX_SUB // SUBLANES)
    pos_seg = pos_seg.transpose(0, 1, 3, 2).reshape(bsz, seq, 1)
    dmask, q_dec, k_dec, state_decay = _retention_tables(MIX_SUB, times)
    row = lambda v: v.reshape(1, -1)

    params = pltpu.CompilerParams(dimension_semantics=("arbitrary", "arbitrary"),
                                  vmem_limit_bytes=VMEM_LIMIT_BYTES)
    tile_spec = lambda t: pl.BlockSpec((None, t, D_MODEL), lambda b, s: (b, s, 0))

    for l in range(depth):
        w_ri = jnp.concatenate([lru_w_r[l], lru_w_i[l]], axis=-1).astype(BF16)
        mixer_consts = [
            jnp.asarray(perm, BF16), jnp.asarray(perm.T, BF16),
            inv_freq2, row(norm1_w[l]), w_in[l].astype(BF16), merge_gate_b[l], row(ret_gn_w[l]),
            w_ret_o[l].astype(BF16), lru_conv_w[l], row(lru_conv_b[l]), w_ri,
            row(lru_b_r[l]), row(lru_b_i[l]), row(lru_lambda[l]), w_lru_o[l].astype(BF16),
            w_out[l].astype(BF16), jnp.asarray(dmask), jnp.asarray(q_dec), jnp.asarray(k_dec)]
        x = pl.pallas_call(
            functools.partial(_mixer_kernel, state_decay=state_decay),
            grid=(bsz, seq // MIX_TILE),
            in_specs=[tile_spec(MIX_TILE),
                      pl.BlockSpec((None, MIX_TILE, 1), lambda b, s: (b, s, 0))]
                     + [_const_spec(c.shape) for c in mixer_consts],
            out_specs=tile_spec(MIX_TILE),
            out_shape=jax.ShapeDtypeStruct(x.shape, x.dtype),
            scratch_shapes=[pltpu.VMEM((RET_HEADS, RET_DK, RET_DV), F32),
                            pltpu.VMEM((SUBLANES, LRU_WIDTH), F32),
                            pltpu.VMEM((SUBLANES * (LRU_CONV - 1), LRU_WIDTH), F32)],
            compiler_params=params,
            name="token_mixer",
        )(x, pos_seg, *mixer_consts)

        ffn_consts = [row(norm2_w[l]), ffn_w_up[l].astype(BF16), ffn_conv_w[l], row(ffn_conv_b[l]),
                      ffn_w_down[l].astype(BF16), row(norm_f_w)]
        x = pl.pallas_call(
            functools.partial(_ffn_kernel, final_norm=(l == depth - 1)),
            grid=(bsz, seq // FFN_TILE),
            in_specs=[tile_spec(FFN_TILE)] + [_const_spec(c.shape) for c in ffn_consts],
            out_specs=tile_spec(FFN_TILE),
            out_shape=jax.ShapeDtypeStruct(x.shape, x.dtype),
            scratch_shapes=[pltpu.VMEM((SUBLANES, D_FF), F32)],
            compiler_params=params,
            name="channel_mixer",
        )(x, *ffn_consts)
    return x
```

```python
import functools

import jax
import jax.numpy as jnp
import numpy as np
from jax import lax
from jax.experimental import pallas as pl
from jax.experimental.pallas import tpu as pltpu

D_MODEL = 1024
CHUNK = 64
RET_HEADS = 4
RET_DK = 128
RET_DV = 256
RET_QK_WIDTH = RET_HEADS * RET_DK
RET_V_WIDTH = RET_HEADS * RET_DV
LRU_WIDTH = 1024
LRU_BLOCKS = 4
LRU_BLOCK = LRU_WIDTH // LRU_BLOCKS
LRU_CONV = 4
LRU_C = 8.0
D_FF = 3 * D_MODEL
FFN_CONV = 3
ROPE_BASE = 10000.0
RMS_EPS = 1e-6
GN_EPS = 1e-6

Q_OFF = 0
K_OFF = Q_OFF + RET_QK_WIDTH
V_OFF = K_OFF + RET_QK_WIDTH
G_OFF = V_OFF + RET_V_WIDTH
XL_OFF = G_OFF + RET_V_WIDTH
YL_OFF = XL_OFF + LRU_WIDTH
GR_OFF = YL_OFF + LRU_WIDTH
GL_OFF = GR_OFF + D_MODEL
D_IN = GL_OFF + D_MODEL

SUBLANES = 8
LANES = 128
MIX_SUB = 256
MIX_TILE = 512
FFN_TILE = 1024
FFN_COLS = D_FF
VMEM_LIMIT_BYTES = 56 * 1024 * 1024

F32 = jnp.float32
BF16 = jnp.bfloat16


def _dot(a, b):
    return jnp.dot(a, b, preferred_element_type=F32)


def _sigmoid(x):
    return 1.0 / (1.0 + jnp.exp(-x))


def _rms_norm(x, w):
    ms = jnp.mean(x * x, axis=-1, keepdims=True)
    return x * lax.rsqrt(ms + RMS_EPS) * w


def _group(v, j):
    return v[j * SUBLANES:(j + 1) * SUBLANES, :]


def _sublane_index(width):
    return lax.broadcasted_iota(jnp.int32, (SUBLANES, width), 0)


def _shift_rows(cur, prev_tail, k):
    rolled = pltpu.roll(cur, k, axis=0)
    head = jnp.where(_sublane_index(cur.shape[1]) < k, pltpu.roll(prev_tail, k, axis=0),
                     rolled[:SUBLANES])
    return jnp.concatenate([head, rolled[SUBLANES:]], axis=0)


def _shift_segments(cur, prev_tail, k):
    n_rows, width = cur.shape
    n_prev = prev_tail.shape[0]
    first = _sublane_index(width) == 0
    heads = []
    for g in range(k):
        src = cur[n_rows - SUBLANES * (k - g):n_rows - SUBLANES * (k - g - 1), :]
        prev = prev_tail[n_prev - SUBLANES * (k - g):n_prev - SUBLANES * (k - g - 1), :]
        heads.append(jnp.where(first, pltpu.roll(prev, 1, axis=0), pltpu.roll(src, 1, axis=0)))
    return jnp.concatenate(heads + [cur[:n_rows - SUBLANES * k, :]], axis=0)


def _causal_conv(cur, prev_tail, w_ref, b_ref, shift):
    k_width = w_ref.shape[0]
    acc = cur * w_ref[k_width - 1:k_width, :] + b_ref[...]
    for k in range(1, k_width):
        acc = acc + shift(cur, prev_tail, k) * w_ref[k_width - 1 - k:k_width - k, :]
    return acc


def _segment_scan(a, b, h_in):
    n_groups = a.shape[0] // SUBLANES
    width = a.shape[1]
    h = jnp.zeros((SUBLANES, width), F32)
    p = jnp.ones((SUBLANES, width), F32)
    for j in range(n_groups):
        a_j = _group(a, j)
        h = a_j * h + _group(b, j)
        p = p * a_j
    sub = _sublane_index(width)
    shift = 1
    while shift < SUBLANES:
        keep = sub >= shift
        p_prev = jnp.where(keep, pltpu.roll(p, shift, axis=0), 1.0)
        h_prev = jnp.where(keep, pltpu.roll(h, shift, axis=0), 0.0)
        h = p * h_prev + h
        p = p * p_prev
        shift *= 2
    seg_end = h + p * h_in
    h = jnp.where(sub == 0, h_in, pltpu.roll(seg_end, 1, axis=0))
    out = []
    for j in range(n_groups):
        h = _group(a, j) * h + _group(b, j)
        out.append(h)
    return jnp.concatenate(out, axis=0), seg_end[SUBLANES - 1:, :]


def _lane_rows_to_column(rows):
    lanes = rows.shape[1]
    diagonal = (lax.broadcasted_iota(jnp.int32, (lanes, lanes), 0)
                == lax.broadcasted_iota(jnp.int32, (lanes, lanes), 1))
    cols = [jnp.sum(jnp.where(diagonal, rows[r:r + 1, :], 0.0), axis=1, keepdims=True)
            for r in range(rows.shape[0])]
    return jnp.concatenate(cols, axis=0)


def _rotary_tables(pos, inv_freq2):
    half_rows = pos.shape[0] // 2
    half = inv_freq2.shape[1] // 2
    lane = lax.broadcasted_iota(jnp.int32, (half_rows, inv_freq2.shape[1]), 1)
    low = lane < half
    ang = jnp.where(low, pos[:half_rows, :], pos[half_rows:, :]) * inv_freq2
    cos = jnp.cos(ang)
    sin = jnp.sin(ang)
    cos_swapped = pltpu.roll(cos, half, axis=1)
    sin_swapped = pltpu.roll(sin, half, axis=1)
    cos = jnp.concatenate([jnp.where(low, cos, cos_swapped), jnp.where(low, cos_swapped, cos)], axis=0)
    sin = jnp.concatenate([jnp.where(low, -sin, sin_swapped), jnp.where(low, -sin_swapped, sin)], axis=0)
    return cos, sin


def _zero_after(*values):
    acc = None
    for v in values:
        v = v.astype(F32)
        folded = v[:SUBLANES, :]
        for j in range(1, v.shape[0] // SUBLANES):
            folded = jnp.maximum(folded, _group(v, j))
        for c in range(folded.shape[1] // LANES):
            part = folded[:, c * LANES:(c + 1) * LANES]
            acc = part if acc is None else jnp.maximum(acc, part)
    bits = pltpu.bitcast(acc, jnp.uint32)
    bits = lax.shift_right_logical(lax.shift_right_logical(bits, jnp.uint32(16)), jnp.uint32(16))
    return pltpu.bitcast(bits, F32)


def _mixer_prepare(x_ref, pos_ref, perm_ref, invf_ref, n1w_ref, h_ref, cos_ref, sin_ref):
    subs = [slice(u * MIX_SUB, (u + 1) * MIX_SUB) for u in range(x_ref.shape[0] // MIX_SUB)]
    pos = _lane_rows_to_column(pos_ref[...].astype(F32))
    h = _rms_norm(x_ref[...], n1w_ref[...]).astype(BF16)
    made = []
    for rows in subs:
        cos, sin = _rotary_tables(pos[rows, :], invf_ref[...])
        cos_ref[rows, :] = cos
        sin_ref[rows, :] = sin
        h_seg = _dot(perm_ref[...], h[rows, :])
        h_ref[rows, :] = h_seg.astype(BF16)
        made += [cos, sin, h_seg]
    return _zero_after(*made)


def _mixer_kernel(x_ref, pos_ref, xnext_ref, posnext_ref, perm_ref, unperm_ref, invf_ref, n1w_ref,
                  win_ref, mgb_ref, gnw_ref, wro_ref, cw_ref, cb_ref, wri_ref, br_ref, bi_ref,
                  lam_ref, wlo_ref, wout_ref, dmask_ref, qdec_ref, kdec_ref, o_ref,
                  state_ref, hcar_ref, xcar_ref, h_ref, cos_ref, sin_ref, *, state_decay):
    @pl.when(pl.program_id(1) == 0)
    def _():
        state_ref[...] = jnp.zeros_like(state_ref)
        hcar_ref[...] = jnp.zeros_like(hcar_ref)
        xcar_ref[...] = jnp.zeros_like(xcar_ref)

    @pl.when((pl.program_id(0) == 0) & (pl.program_id(1) == 0))
    def _():
        _mixer_prepare(x_ref, pos_ref, perm_ref, invf_ref, n1w_ref, h_ref, cos_ref, sin_ref)

    subs = [slice(u * MIX_SUB, (u + 1) * MIX_SUB) for u in range(x_ref.shape[0] // MIX_SUB)]
    h = h_ref[...]
    cos = cos_ref[...]
    sin = sin_ref[...]

    def proj(off, width):
        return _dot(h, win_ref[:, off:off + width])

    qk = proj(Q_OFF, 2 * RET_QK_WIDTH)
    prepared = _mixer_prepare(xnext_ref, posnext_ref, perm_ref, invf_ref, n1w_ref, h_ref, cos_ref, sin_ref)

    states = [state_ref[hd] for hd in range(RET_HEADS)]
    h_in = hcar_ref[0:1, :]
    x_tail = xcar_ref[...]
    new_states = [None] * RET_HEADS
    a_parts = [None] * RET_HEADS
    b_result = {}

    def retention():
        qs, ks = [], []
        for hd in range(RET_HEADS):
            qh = qk[:, hd * RET_DK:(hd + 1) * RET_DK]
            kh = qk[:, RET_QK_WIDTH + hd * RET_DK:RET_QK_WIDTH + (hd + 1) * RET_DK]
            qs.append(qh * cos + pltpu.roll(qh, RET_DK // 2, axis=1) * sin)
            ks.append((kh * cos + pltpu.roll(kh, RET_DK // 2, axis=1) * sin) * (RET_DK ** -0.5))
        v = proj(V_OFF, RET_V_WIDTH).astype(BF16)
        yield
        scores, q_dec, k_dec = [], [], []
        for hd in range(RET_HEADS):
            qb = qs[hd].astype(BF16)
            kb = ks[hd].astype(BF16)
            dk_cols = slice(hd * RET_DK, (hd + 1) * RET_DK)
            scores.append([lax.dot_general(qb[rows, :], kb[rows, :], (((1,), (1,)), ((), ())),
                                           preferred_element_type=F32) for rows in subs])
            q_dec.append([(qs[hd][rows, :] * qdec_ref[:, dk_cols]).astype(BF16) for rows in subs])
            k_dec.append([(ks[hd][rows, :] * kdec_ref[:, dk_cols]).astype(BF16) for rows in subs])
        yield
        scores = [[(sc * dmask_ref[hd]).astype(BF16) for sc in scores[hd]] for hd in range(RET_HEADS)]
        g = proj(G_OFF, RET_V_WIDTH)
        yield
        outs = []
        for hd in range(RET_HEADS):
            vh = v[:, hd * RET_DV:(hd + 1) * RET_DV]
            state = states[hd]
            parts = []
            for u, rows in enumerate(subs):
                parts.append(_dot(scores[hd][u], vh[rows, :]) + _dot(q_dec[hd][u], state.astype(BF16)))
                state = state * state_decay[hd] + lax.dot_general(
                    k_dec[hd][u], vh[rows, :], (((0,), (0,)), ((), ())), preferred_element_type=F32)
            new_states[hd] = state
            outs.append(jnp.concatenate(parts, axis=0))
        swish = g * _sigmoid(g)
        yield
        for hd in range(RET_HEADS):
            oh = outs[hd]
            mu = jnp.mean(oh, axis=-1, keepdims=True)
            dev = oh - mu
            var = jnp.mean(dev * dev, axis=-1, keepdims=True)
            cols = slice(hd * RET_DV, (hd + 1) * RET_DV)
            a_parts[hd] = (dev * lax.rsqrt(var + GN_EPS) * gnw_ref[:, cols] * swish[:, cols]).astype(BF16)

    def recurrence():
        xl = proj(XL_OFF, LRU_WIDTH)
        yield
        tail = x_tail
        xc = []
        for rows in subs:
            cur = xl[rows, :]
            xc.append(_causal_conv(cur, tail, cw_ref, cb_ref, _shift_segments))
            tail = cur[MIX_SUB - x_tail.shape[0]:, :]
        b_result['tail'] = tail
        xc = jnp.concatenate(xc, axis=0)
        yield
        ri = [_dot(xc[:, blk * LRU_BLOCK:(blk + 1) * LRU_BLOCK].astype(BF16), wri_ref[blk])
              for blk in range(LRU_BLOCKS)]
        yl = proj(YL_OFF, LRU_WIDTH)
        yield
        pinned = jnp.concatenate([ri[0][:SUBLANES, :LANES] + prepared, ri[0][:SUBLANES, LANES:]], axis=1)
        ri[0] = jnp.concatenate([pinned, ri[0][SUBLANES:, :]], axis=0)
        r = _sigmoid(jnp.concatenate([blk[:, :LRU_BLOCK] for blk in ri], axis=1) + br_ref[...])
        i = _sigmoid(jnp.concatenate([blk[:, LRU_BLOCK:] for blk in ri], axis=1) + bi_ref[...])
        neg_lam = -lam_ref[...]
        softplus = jnp.maximum(neg_lam, 0.0) + jnp.log1p(jnp.exp(-jnp.abs(neg_lam)))
        a = jnp.exp((-LRU_C * r) * softplus)
        one_minus_a2 = 1.0 - a * a
        root = jnp.where(one_minus_a2 > 0.0, one_minus_a2 * lax.rsqrt(one_minus_a2), 0.0)
        b = root * (i * xc)
        yield
        carry = h_in
        hl = []
        for rows in subs:
            h_sub, carry = _segment_scan(a[rows, :], b[rows, :], carry)
            hl.append(h_sub)
        b_result['h_last'] = carry
        b_result['act'] = (jnp.concatenate(hl, axis=0) * jax.nn.gelu(yl)).astype(BF16)

    chains = [retention(), recurrence()]
    while chains:
        for chain in list(chains):
            if next(chain, chain) is chain:
                chains.remove(chain)
    a_act = jnp.concatenate(a_parts, axis=1)
    b_act = b_result['act']
    for hd in range(RET_HEADS):
        state_ref[hd] = new_states[hd]
    hcar_ref[0:1, :] = b_result['h_last']
    xcar_ref[...] = b_result['tail']

    gates = proj(GR_OFF, 2 * D_MODEL)
    gate_a = _sigmoid(gates[:, :D_MODEL] + mgb_ref[0:1, :])
    gate_b = _sigmoid(gates[:, D_MODEL:] + mgb_ref[1:2, :])
    mix = (gate_a * _dot(a_act, wro_ref[...]) + gate_b * _dot(b_act, wlo_ref[...])).astype(BF16)
    mix = jnp.concatenate([_dot(unperm_ref[...], mix[rows, :]).astype(BF16) for rows in subs], axis=0)
    o_ref[...] = x_ref[...] + _dot(mix, wout_ref[...])


def _ffn_kernel(x_ref, n2w_ref, wup_ref, cw_ref, cb_ref, wdn_ref, nfw_ref, o_ref, gcar_ref, *,
                final_norm):
    @pl.when(pl.program_id(1) == 0)
    def _():
        gcar_ref[...] = jnp.zeros_like(gcar_ref)

    x = x_ref[...]
    h = _rms_norm(x, n2w_ref[...]).astype(BF16)
    acc = x
    for c in range(D_FF // FFN_COLS):
        cols = slice(c * FFN_COLS, (c + 1) * FFN_COLS)
        gate = _dot(h, wup_ref[:, cols])
        val = _dot(h, wup_ref[:, D_FF + c * FFN_COLS:D_FF + (c + 1) * FFN_COLS])
        conv = _causal_conv(gate, gcar_ref[:, cols], cw_ref.at[:, cols], cb_ref.at[:, cols],
                            _shift_rows)
        gcar_ref[:, cols] = gate[gate.shape[0] - SUBLANES:, :]
        acc = acc + _dot((jax.nn.gelu(conv) * val).astype(BF16), wdn_ref[cols, :])
    o_ref[...] = _rms_norm(acc, nfw_ref[...]) if final_norm else acc


def _const_spec(shape):
    zeros = (0,) * len(shape)
    return pl.BlockSpec(shape, lambda b, s: zeros, pipeline_mode=pl.Buffered(1))


def _segment_times(tile):
    rows = np.arange(tile)
    return (rows % SUBLANES) * (tile // SUBLANES) + rows // SUBLANES


def _retention_tables(tile, times):
    hd = np.arange(RET_HEADS, dtype=np.float64)
    log_gamma = np.log1p(-np.power(2.0, -5.0 - hd))
    idx = times.astype(np.float64)
    diff = idx[:, None] - idx[None, :]
    chunk = times // CHUNK
    same = chunk[:, None] == chunk[None, :]
    earlier = chunk[None, :] < chunk[:, None]
    expo = np.where(same, np.abs(diff), diff)
    dmask = np.where(same | earlier, np.exp(log_gamma[:, None, None] * expo), 0.0)
    q_dec = np.exp(log_gamma[None, :] * (idx[:, None] + 1.0))
    k_dec = np.exp(log_gamma[None, :] * (tile - 1.0 - idx[:, None]))
    state_decay = tuple(float(v) for v in np.exp(log_gamma * tile))
    widen = lambda t: np.repeat(t, RET_DK, axis=1).astype(np.float32)
    return dmask.astype(np.float32), widen(q_dec), widen(k_dec), state_decay


def kernel(x, positions, norm1_w, w_in, merge_gate_b, ret_gn_w, w_ret_o, lru_conv_w, lru_conv_b,
           lru_w_r, lru_b_r, lru_w_i, lru_b_i, lru_lambda, w_lru_o, w_out, norm2_w, ffn_w_up,
           ffn_conv_w, ffn_conv_b, ffn_w_down, norm_f_w):
    bsz, seq, d_model = x.shape
    depth = w_in.shape[0]
    assert d_model == D_MODEL and seq % MIX_TILE == 0 and seq % FFN_TILE == 0
    assert MIX_TILE % MIX_SUB == 0 and MIX_SUB % CHUNK == 0 and MIX_SUB // SUBLANES >= LRU_CONV - 1

    half = RET_DK // 2
    inv_freq = ROPE_BASE ** (-jnp.arange(half, dtype=F32) / half)
    inv_freq2 = jnp.concatenate([inv_freq, inv_freq])[None, :]
    times = _segment_times(MIX_SUB)
    perm = np.zeros((MIX_SUB, MIX_SUB), np.float32)
    perm[np.arange(MIX_SUB), times] = 1.0
    pos_seg = positions.reshape(bsz, seq // MIX_SUB, SUBLANES, MIX_SUB // SUBLANES)
    pos_seg = pos_seg.transpose(0, 1, 3, 2).reshape(bsz, seq // MIX_TILE, MIX_TILE // LANES, LANES)
    dmask, q_dec, k_dec, state_decay = _retention_tables(MIX_SUB, times)
    row = lambda v: v.reshape(1, -1)

    params = pltpu.CompilerParams(dimension_semantics=("arbitrary", "arbitrary"),
                                  vmem_limit_bytes=VMEM_LIMIT_BYTES)
    tile_spec = lambda t: pl.BlockSpec((None, t, D_MODEL), lambda b, s: (b, s, 0))
    n_mix = seq // MIX_TILE

    def next_mix_tile(b, s):
        step = jnp.minimum(b * n_mix + s + 1, bsz * n_mix - 1)
        return step // n_mix, step % n_mix

    for l in range(depth):
        w_ri = jnp.concatenate([lru_w_r[l], lru_w_i[l]], axis=-1).astype(BF16)
        mixer_consts = [
            jnp.asarray(perm, BF16), jnp.asarray(perm.T, BF16),
            inv_freq2, row(norm1_w[l]), w_in[l].astype(BF16), merge_gate_b[l], row(ret_gn_w[l]),
            w_ret_o[l].astype(BF16), lru_conv_w[l], row(lru_conv_b[l]), w_ri,
            row(lru_b_r[l]), row(lru_b_i[l]), row(lru_lambda[l]), w_lru_o[l].astype(BF16),
            w_out[l].astype(BF16), jnp.asarray(dmask), jnp.asarray(q_dec), jnp.asarray(k_dec)]
        x = pl.pallas_call(
            functools.partial(_mixer_kernel, state_decay=state_decay),
            grid=(bsz, seq // MIX_TILE),
            in_specs=[tile_spec(MIX_TILE),
                      pl.BlockSpec((None, None, MIX_TILE // LANES, LANES), lambda b, s: (b, s, 0, 0)),
                      pl.BlockSpec((None, MIX_TILE, D_MODEL), lambda b, s: (*next_mix_tile(b, s), 0)),
                      pl.BlockSpec((None, None, MIX_TILE // LANES, LANES),
                                   lambda b, s: (*next_mix_tile(b, s), 0, 0))]
                     + [_const_spec(c.shape) for c in mixer_consts],
            out_specs=tile_spec(MIX_TILE),
            out_shape=jax.ShapeDtypeStruct(x.shape, x.dtype),
            scratch_shapes=[pltpu.VMEM((RET_HEADS, RET_DK, RET_DV), F32),
                            pltpu.VMEM((SUBLANES, LRU_WIDTH), F32),
                            pltpu.VMEM((SUBLANES * (LRU_CONV - 1), LRU_WIDTH), F32),
                            pltpu.VMEM((MIX_TILE, D_MODEL), BF16),
                            pltpu.VMEM((MIX_TILE, RET_DK), F32),
                            pltpu.VMEM((MIX_TILE, RET_DK), F32)],
            compiler_params=params,
            name="token_mixer",
        )(x, pos_seg, x, pos_seg, *mixer_consts)

        ffn_consts = [row(norm2_w[l]), ffn_w_up[l].astype(BF16), ffn_conv_w[l], row(ffn_conv_b[l]),
                      ffn_w_down[l].astype(BF16), row(norm_f_w)]
        x = pl.pallas_call(
            functools.partial(_ffn_kernel, final_norm=(l == depth - 1)),
            grid=(bsz, seq // FFN_TILE),
            in_specs=[tile_spec(FFN_TILE)] + [_const_spec(c.shape) for c in ffn_consts],
            out_specs=tile_spec(FFN_TILE),
            out_shape=jax.ShapeDtypeStruct(x.shape, x.dtype),
            scratch_shapes=[pltpu.VMEM((SUBLANES, D_FF), F32)],
            compiler_params=params,
            name="channel_mixer",
        )(x, *ffn_consts)
    return x
```
